```python
import math
import jax, jax.numpy as jnp
from jax import lax
import numpy as np

D_MODEL = 4096
BATCH = 16
SEQ = 256
DEPTH = 2
DEC_BATCH = 8
DEC_SEQ = 2048
PAST_LEN = 256

GRID_W = 64
HEAD_DIM = 128
Q_BLOCK = 128
WINDOW = 128
ROPE_BASE = 10000.0
EPS = 1e-6
NEG_INF = -1e30

A_HEADS = 24
A_KV_HEADS = 6
A_GROUP = A_HEADS // A_KV_HEADS
POOL_WINDOWS = (2, 4, 8, 16)
POOL_GROUPS = len(POOL_WINDOWS)
POOL_GROUP_DIM = 256
POOL_DIM = POOL_GROUPS * POOL_GROUP_DIM
EVEN_IN = (A_HEADS + 2 * A_KV_HEADS) * HEAD_DIM + POOL_DIM
EVEN_MIX = A_HEADS * HEAD_DIM + POOL_DIM
C_HEADS = 16
C_Q_RANK = 1024
C_KV_RANK = 512
C_NOPE = 128
C_ROPE = 64
C_V = 128
D_HEADS = 8
D_DK = 128
D_DV = 2 * D_DK
ODD_IN = C_Q_RANK + C_KV_RANK + C_ROPE + 2 * D_HEADS * 2 * D_DK + D_HEADS * D_DV
ODD_MIX = C_HEADS * C_V + D_HEADS * D_DV
FF_DIM = 11008
N_EVEN = (DEPTH + 1) // 2
N_ODD = DEPTH // 2

kernel_name = 'hybrid_prefix_diffusion_step'


def rms_norm(x, g):
    xf = x.astype(jnp.float32)
    y = xf * lax.rsqrt(jnp.mean(xf * xf, axis=-1, keepdims=True) + EPS)
    return (y * g.astype(jnp.float32)).astype(x.dtype)


def _rope_1d(x, pos):
    half = x.shape[-1] // 2
    inv = ROPE_BASE ** (-jnp.arange(half, dtype=jnp.float32) / half)
    ang = pos.astype(jnp.float32)[:, None] * inv[None, :]
    cos = jnp.cos(ang)[None, :, None, :]
    sin = jnp.sin(ang)[None, :, None, :]
    xf = x.astype(jnp.float32)
    x1, x2 = xf[..., :half], xf[..., half:]
    return jnp.concatenate([x1 * cos - x2 * sin, x2 * cos + x1 * sin], axis=-1).astype(x.dtype)


def rope_2d(x):
    S = x.shape[1]
    rows = S // GRID_W
    row = jnp.repeat(jnp.arange(rows), GRID_W)
    col = jnp.tile(jnp.arange(GRID_W), rows)
    d = x.shape[-1] // 2
    return jnp.concatenate([_rope_1d(x[..., :d], row), _rope_1d(x[..., d:], col)], axis=-1)


def to_blocks(x):
    B, S = x.shape[:2]
    return jnp.moveaxis(x.reshape(B, S // Q_BLOCK, Q_BLOCK, *x.shape[2:]), 1, 0)


def from_blocks(y):
    n, B, qb = y.shape[:3]
    return jnp.moveaxis(y, 0, 1).reshape(B, n * qb, *y.shape[3:])


def band_windows(x):
    B, S = x.shape[:2]
    n = S // Q_BLOCK
    xp = jnp.pad(x, [(0, 0), (Q_BLOCK, Q_BLOCK)] + [(0, 0)] * (x.ndim - 2))
    xb = jnp.moveaxis(xp.reshape(B, n + 2, Q_BLOCK, *x.shape[2:]), 1, 0)
    return jnp.concatenate([xb[:n], xb[1:n + 1], xb[2:]], axis=2)


def softmax_parts(logits, sink=None):
    sizes = [l.shape[-1] for l in logits]
    parts = list(logits)
    if sink is not None:
        parts.append(jnp.broadcast_to(sink, logits[0].shape[:-1] + (1,)))
    p = jax.nn.softmax(jnp.concatenate(parts, axis=-1), axis=-1)
    out, off = [], 0
    for s in sizes:
        out.append(p[..., off:off + s])
        off += s
    return out


def window_gqa_context(q, k, v, sink):
    scale = HEAD_DIM ** -0.5
    sink_l = sink.astype(jnp.float32).reshape(1, A_KV_HEADS, A_GROUP, 1, 1)

    def block(qb):
        s = jnp.einsum('bqkgd,bckd->bkgqc', qb, k, preferred_element_type=jnp.float32) * scale
        (p,) = softmax_parts([s], sink_l)
        return jnp.einsum('bkgqc,bckd->bqkgd', p.astype(v.dtype), v)

    return from_blocks(lax.map(block, to_blocks(q)))


def window_gqa_latent(q, k, v, sink, ctx_k, ctx_v):
    S = q.shape[1]
    scale = HEAD_DIM ** -0.5
    sink_l = sink.astype(jnp.float32).reshape(1, A_KV_HEADS, A_GROUP, 1, 1)
    kw, vw = band_windows(k), band_windows(v)
    kvalid = band_windows(jnp.ones((1, S), dtype=bool))[:, 0]
    rel = jnp.arange(3 * Q_BLOCK)[None, :] - Q_BLOCK - jnp.arange(Q_BLOCK)[:, None]
    band = jnp.abs(rel) <= WINDOW

    def block(args):
        qb, kb, vb, valid = args
        s_lat = jnp.einsum('bqkgd,bjkd->bkgqj', qb, kb, preferred_element_type=jnp.float32) * scale
        s_lat = jnp.where(band & valid[None, :], s_lat, NEG_INF)
        s_ctx = jnp.einsum('bqkgd,bckd->bkgqc', qb, ctx_k, preferred_element_type=jnp.float32) * scale
        p_lat, p_ctx = softmax_parts([s_lat, s_ctx], sink_l)
        return (jnp.einsum('bkgqj,bjkd->bqkgd', p_lat.astype(vb.dtype), vb)
                + jnp.einsum('bkgqc,bckd->bqkgd', p_ctx.astype(ctx_v.dtype), ctx_v))

    return from_blocks(lax.map(block, (to_blocks(q), kw, vw, kvalid)))


def pool_mixer(u, w_pool, pool_scale):
    B, S, _ = u.shape
    ug = u.reshape(B, S, POOL_GROUPS, POOL_GROUP_DIM)
    cs = jnp.pad(jnp.cumsum(ug.astype(jnp.float32), axis=1), ((0, 0), (1, 0), (0, 0), (0, 0)))
    half = jnp.array(POOL_WINDOWS, dtype=jnp.int32) // 2
    t = jnp.arange(S, dtype=jnp.int32)
    lo = jnp.clip(t[:, None] - half[None, :], 0, S)
    hi = jnp.clip(t[:, None] + half[None, :], 0, S)
    gidx = jnp.arange(POOL_GROUPS)[None, :]
    total = cs[:, hi, gidx] - cs[:, lo, gidx]
    count = (hi - lo).astype(jnp.float32)[None, :, :, None]
    pooled = (total / count - ug.astype(jnp.float32)).astype(u.dtype)
    y = jnp.einsum('bsgc,gcd->bsgd', pooled, w_pool)
    return y.reshape(B, S, POOL_DIM) * pool_scale


def even_mixer(h, w_in, w_out, sink, w_pool, pool_scale, ctx):
    B, S, _ = h.shape
    qa, kva = A_HEADS * HEAD_DIM, A_KV_HEADS * HEAD_DIM
    q, k, v, u = jnp.split(h @ w_in, [qa, qa + kva, qa + 2 * kva], axis=-1)
    q = q.reshape(B, S, A_HEADS, HEAD_DIM)
    k = k.reshape(B, S, A_KV_HEADS, HEAD_DIM)
    v = v.reshape(B, S, A_KV_HEADS, HEAD_DIM)
    if ctx is None:
        a = window_gqa_context(q.reshape(B, S, A_KV_HEADS, A_GROUP, HEAD_DIM), k, v, sink)
    else:
        q = rope_2d(q)
        k = rope_2d(k)
        a = window_gqa_latent(q.reshape(B, S, A_KV_HEADS, A_GROUP, HEAD_DIM), k, v, sink, ctx[0], ctx[1])
    b = pool_mixer(u, w_pool, pool_scale)
    y = jnp.concatenate([a.reshape(B, S, qa), b], axis=-1) @ w_out
    return y, (k, v)


def mla_attend(cq, ckv, kr, g_qn, w_uq, w_uk, w_uv, ctx_ckv, ctx_kr):
    B, S, _ = cq.shape
    q = (rms_norm(cq, g_qn) @ w_uq).reshape(B, S, C_HEADS, C_NOPE + C_ROPE)
    q_nope, q_rope = q[..., :C_NOPE], q[..., C_NOPE:]
    if ctx_ckv is not None:
        q_rope = rope_2d(q_rope)
        kr = rope_2d(kr[:, :, None, :])[:, :, 0]
    keysets = [(jnp.einsum('bsl,lhd->bshd', ckv, w_uk), kr, jnp.einsum('bsl,lhd->bshd', ckv, w_uv))]
    if ctx_ckv is not None:
        keysets.append((jnp.einsum('bsl,lhd->bshd', ctx_ckv, w_uk), ctx_kr,
                        jnp.einsum('bsl,lhd->bshd', ctx_ckv, w_uv)))
    scale = (C_NOPE + C_ROPE) ** -0.5

    def block(args):
        qn, qr = args
        logits = [(jnp.einsum('bqhd,bkhd->bhqk', qn, kn, preferred_element_type=jnp.float32)
                   + jnp.einsum('bqhr,bkr->bhqk', qr, kre, preferred_element_type=jnp.float32)) * scale
                  for kn, kre, _ in keysets]
        ps = softmax_parts(logits)
        out = jnp.einsum('bhqk,bkhd->bqhd', ps[0].astype(keysets[0][2].dtype), keysets[0][2])
        for p, (_, _, vv) in zip(ps[1:], keysets[1:]):
            out = out + jnp.einsum('bhqk,bkhd->bqhd', p.astype(vv.dtype), vv)
        return out

    return from_blocks(lax.map(block, (to_blocks(q_nope), to_blocks(q_rope))))


def diff_attend(q, k, v, lam, ctx_k, ctx_v):
    B, S = q.shape[:2]
    if ctx_k is not None:
        q = rope_2d(q.reshape(B, S, D_HEADS * 2, D_DK)).reshape(B, S, D_HEADS, 2, D_DK)
        k = rope_2d(k.reshape(B, S, D_HEADS * 2, D_DK)).reshape(B, S, D_HEADS, 2, D_DK)
    keysets = [(k, v)]
    if ctx_k is not None:
        keysets.append((ctx_k, ctx_v))
    scale = D_DK ** -0.5

    def block(qb):
        logits = [jnp.einsum('bqhmd,bkhmd->bhmqk', qb, kk, preferred_element_type=jnp.float32) * scale
                  for kk, _ in keysets]
        ps = softmax_parts(logits)
        out = None
        for p, (_, vv) in zip(ps, keysets):
            w = (p[:, :, 0] - lam * p[:, :, 1]).astype(vv.dtype)
            o = jnp.einsum('bhqk,bkhe->bqhe', w, vv)
            out = o if out is None else out + o
        return out

    return from_blocks(lax.map(block, to_blocks(q)))


def odd_mixer(h, w_in, w_out, g_qn, w_uq, g_kvn, w_uk, w_uv, lq1, lk1, lq2, lk2, g_subln, lam_init, ctx):
    B, S, _ = h.shape
    dqk = D_HEADS * 2 * D_DK
    c0 = C_Q_RANK
    c1 = c0 + C_KV_RANK
    c2 = c1 + C_ROPE
    cq, ckv, kr, dq, dk, dv = jnp.split(h @ w_in, [c0, c1, c2, c2 + dqk, c2 + 2 * dqk], axis=-1)
    ckv = rms_norm(ckv, g_kvn)
    dq = dq.reshape(B, S, D_HEADS, 2, D_DK)
    dk = dk.reshape(B, S, D_HEADS, 2, D_DK)
    dv = dv.reshape(B, S, D_HEADS, D_DV)
    lam = (jnp.exp(jnp.sum(lq1.astype(jnp.float32) * lk1.astype(jnp.float32)))
           - jnp.exp(jnp.sum(lq2.astype(jnp.float32) * lk2.astype(jnp.float32))) + lam_init)
    if ctx is None:
        c_out = mla_attend(cq, ckv, kr, g_qn, w_uq, w_uk, w_uv, None, None)
        d_out = diff_attend(dq, dk, dv, lam, None, None)
    else:
        c_out = mla_attend(cq, ckv, kr, g_qn, w_uq, w_uk, w_uv, ctx[0], ctx[1])
        d_out = diff_attend(dq, dk, dv, lam, ctx[2], ctx[3])
    d_out = rms_norm(d_out, g_subln) * (1.0 - lam_init)
    y = jnp.concatenate([c_out.reshape(B, S, C_HEADS * C_V), d_out.reshape(B, S, D_HEADS * D_DV)], axis=-1) @ w_out
    return y, (ckv, kr, dk, dv)


def conv_ffn(h, w_up, conv_w, conv_b, w_down):
    z = h @ w_up
    zp = jnp.pad(z, ((0, 0), (1, 1), (0, 0)))
    z = zp[:, :-2] * conv_w[0] + zp[:, 1:-1] * conv_w[1] + zp[:, 2:] * conv_w[2] + conv_b
    gate, val = jnp.split(z, 2, axis=-1)
    return (jax.nn.silu(gate) * val) @ w_down


def setup_inputs(seed: int = 0) -> dict:
    key = jax.random.key(seed)
    ks = iter(jax.random.split(key, 40))
    D = D_MODEL

    def nrm(shape, scale):
        return jax.random.normal(next(ks), shape, jnp.float32) * scale

    def gain(shape):
        return 1.0 + nrm(shape, 0.02)

    return {
        'x_prompt': nrm((BATCH, SEQ, D), 1.0),
        'x_sample': nrm((DEC_BATCH, DEC_SEQ, D), 1.0),
        'cache_a_k': nrm((DEC_BATCH, N_EVEN, PAST_LEN, A_KV_HEADS, HEAD_DIM), 1.0),
        'cache_a_v': nrm((DEC_BATCH, N_EVEN, PAST_LEN, A_KV_HEADS, HEAD_DIM), 1.0),
        'cache_c_ckv': nrm((DEC_BATCH, N_ODD, PAST_LEN, C_KV_RANK), 1.0),
        'cache_c_krope': nrm((DEC_BATCH, N_ODD, PAST_LEN, C_ROPE), 1.0),
        'cache_d_k': nrm((DEC_BATCH, N_ODD, PAST_LEN, D_HEADS, 2, D_DK), 1.0),
        'cache_d_v': nrm((DEC_BATCH, N_ODD, PAST_LEN, D_HEADS, D_DV), 1.0),
        'c': nrm((DEC_BATCH, D), 1.0),
        'c_ctx': nrm((D,), 1.0),
        'w_mod': nrm((DEPTH, D, 6 * D), 0.5 * D ** -0.5),
        'b_mod': nrm((DEPTH, 6 * D), 0.01),
        'g_pre_mix': gain((DEPTH, D)),
        'g_post_mix': gain((DEPTH, D)),
        'g_pre_ffn': gain((DEPTH, D)),
        'g_post_ffn': gain((DEPTH, D)),
        'w_in_even': nrm((N_EVEN, D, EVEN_IN), D ** -0.5),
        'w_out_even': nrm((N_EVEN, EVEN_MIX, D), EVEN_MIX ** -0.5),
        'a_sink': nrm((N_EVEN, A_HEADS), 0.5),
        'w_pool': nrm((N_EVEN, POOL_GROUPS, POOL_GROUP_DIM, POOL_GROUP_DIM), POOL_GROUP_DIM ** -0.5),
        'pool_scale': gain((N_EVEN, POOL_DIM)),
        'w_in_odd': nrm((N_ODD, D, ODD_IN), D ** -0.5),
        'w_out_odd': nrm((N_ODD, ODD_MIX, D), ODD_MIX ** -0.5),
        'g_q_norm': gain((N_ODD, C_Q_RANK)),
        'w_uq': nrm((N_ODD, C_Q_RANK, C_HEADS * (C_NOPE + C_ROPE)), C_Q_RANK ** -0.5),
        'g_kv_norm': gain((N_ODD, C_KV_RANK)),
        'w_uk': nrm((N_ODD, C_KV_RANK, C_HEADS, C_NOPE), C_KV_RANK ** -0.5),
        'w_uv': nrm((N_ODD, C_KV_RANK, C_HEADS, C_V), C_KV_RANK ** -0.5),
        'lambda_q1': nrm((N_ODD, D_DK), 0.1),
        'lambda_k1': nrm((N_ODD, D_DK), 0.1),
        'lambda_q2': nrm((N_ODD, D_DK), 0.1),
        'lambda_k2': nrm((N_ODD, D_DK), 0.1),
        'g_subln': gain((N_ODD, D_DV)),
        'w_up': nrm((DEPTH, D, 2 * FF_DIM), D ** -0.5),
        'conv_w': nrm((DEPTH, 3, 2 * FF_DIM), 3 ** -0.5),
        'conv_b': nrm((DEPTH, 2 * FF_DIM), 0.01),
        'w_down': nrm((DEPTH, FF_DIM, D), FF_DIM ** -0.5),
    }


def reference(x_prompt, x_sample, cache_a_k, cache_a_v, cache_c_ckv, cache_c_krope, cache_d_k, cache_d_v,
              c, c_ctx, w_mod, b_mod, g_pre_mix, g_post_mix, g_pre_ffn, g_post_ffn,
              w_in_even, w_out_even, a_sink, w_pool, pool_scale,
              w_in_odd, w_out_odd, g_q_norm, w_uq, g_kv_norm, w_uk, w_uv,
              lambda_q1, lambda_k1, lambda_q2, lambda_k2, g_subln,
              w_up, conv_w, conv_b, w_down):

    def layer(i, x, cond, ctx):
        mods = jax.nn.silu(cond) @ w_mod[i] + b_mod[i]
        sh_m, sc_m, g_m, sh_f, sc_f, g_f = [m[:, None, :] for m in jnp.split(mods, 6, axis=-1)]
        h = rms_norm(x, g_pre_mix[i]) * (1.0 + sc_m) + sh_m
        j = i // 2
        if i % 2 == 0:
            y, st = even_mixer(h, w_in_even[j], w_out_even[j], a_sink[j], w_pool[j], pool_scale[j], ctx)
        else:
            lam_init = 0.8 - 0.6 * math.exp(-0.3 * i)
            y, st = odd_mixer(h, w_in_odd[j], w_out_odd[j], g_q_norm[j], w_uq[j], g_kv_norm[j], w_uk[j], w_uv[j],
                              lambda_q1[j], lambda_k1[j], lambda_q2[j], lambda_k2[j], g_subln[j], lam_init, ctx)
        x = x + g_m * rms_norm(y, g_post_mix[i])
        h = rms_norm(x, g_pre_ffn[i]) * (1.0 + sc_f) + sh_f
        x = x + g_f * rms_norm(conv_ffn(h, w_up[i], conv_w[i], conv_b[i], w_down[i]), g_post_ffn[i])
        return x, st

    xp = x_prompt
    cond_ctx = c_ctx[None, :]
    ak, av, cc, ckr, dkl, dvl = [], [], [], [], [], []
    for i in range(DEPTH):
        xp, st = layer(i, xp, cond_ctx, None)
        if i % 2 == 0:
            ak.append(st[0])
            av.append(st[1])
        else:
            cc.append(st[0])
            ckr.append(st[1])
            dkl.append(st[2])
            dvl.append(st[3])

    xs = x_sample
    for i in range(DEPTH):
        j = i // 2
        if i % 2 == 0:
            ctx = (cache_a_k[:, j], cache_a_v[:, j])
        else:
            ctx = (cache_c_ckv[:, j], cache_c_krope[:, j], cache_d_k[:, j], cache_d_v[:, j])
        xs, _ = layer(i, xs, c, ctx)

    new_a_k = jnp.stack(ak, axis=1)
    new_a_v = jnp.stack(av, axis=1)
    new_c_ckv = jnp.stack(cc, axis=1)
    new_c_krope = jnp.stack(ckr, axis=1)
    new_d_k = jnp.stack(dkl, axis=1)
    new_d_v = jnp.stack(dvl, axis=1)
    return (xp, xs, new_a_k, new_a_v, new_c_ckv, new_c_krope, new_d_k, new_d_v)
```

```python
import functools
import math

import jax
import jax.numpy as jnp
from jax import lax
from jax.experimental import pallas as pl
from jax.experimental.pallas import tpu as pltpu

GRID_W = 64
Q_BLOCK = 128
WINDOW = 128
ROPE_BASE = 10000.0
EPS = 1e-6
NEG_INF = -1e30
POOL_WINDOWS = (2, 4, 8, 16)

LANES = 128
BF16_ROWS = 16
VMEM_LIMIT = 56 * 1024 * 1024

BF16 = jnp.bfloat16
F32 = jnp.float32


def _tile(dim, pref, align):
    best = None
    t = align
    while t <= min(dim, pref):
        if dim % t == 0:
            best = t
        t += align
    return dim if best is None else best


def _params(*sem):
    return pltpu.CompilerParams(dimension_semantics=sem, vmem_limit_bytes=VMEM_LIMIT)


def _mods_kernel(c_ref, w_ref, b_ref, o_ref):
    c = c_ref[...]
    s = (c * (1.0 / (1.0 + jnp.exp(-c)))).astype(BF16)
    o_ref[...] = jnp.dot(s, w_ref[...].astype(BF16), preferred_element_type=F32) + b_ref[...]


def _mods(cond, w_mod, b_mod):
    depth, D, N = w_mod.shape
    R = cond.shape[0]
    tn = _tile(N, 512, LANES)
    return pl.pallas_call(
        _mods_kernel,
        grid=(depth, N // tn),
        in_specs=[
            pl.BlockSpec((R, D), lambda l, j: (0, 0)),
            pl.BlockSpec((None, D, tn), lambda l, j: (l, 0, j)),
            pl.BlockSpec((None, 1, tn), lambda l, j: (l, 0, j)),
        ],
        out_specs=pl.BlockSpec((None, R, tn), lambda l, j: (l, 0, j)),
        out_shape=jax.ShapeDtypeStruct((depth, R, N), F32),
        compiler_params=_params("parallel", "parallel"),
        name="mods",
    )(cond, w_mod, b_mod.reshape(depth, 1, N))


def _rms(x, g):
    return x * lax.rsqrt(jnp.mean(x * x, axis=-1, keepdims=True) + EPS) * g


def _resid_norm_kernel(*refs, has_resid, has_norm):
    refs = list(refs)
    x = refs.pop(0)[0]
    if has_resid:
        y_ref, gpost_ref, gate_ref = refs[:3]
        refs = refs[3:]
        x = x + gate_ref[...] * _rms(y_ref[0], gpost_ref[...])
    if has_norm:
        gpre_ref, sc_ref, sh_ref = refs[:3]
        refs = refs[3:]
    if has_resid:
        refs.pop(0)[0] = x
    if has_norm:
        h = _rms(x, gpre_ref[...]) * (1.0 + sc_ref[...]) + sh_ref[...]
        refs.pop(0)[0] = h.astype(BF16)


def _resid_norm(x, row0, row_stride, *, resid=None, norm=None):
    B, S, D = x.shape
    ts = _tile(S, 256, 8)
    xspec = pl.BlockSpec((1, ts, D), lambda b, s: (b, s, 0))
    gspec = pl.BlockSpec((1, D), lambda b, s: (0, 0))

    def mspec(col):
        return pl.BlockSpec((None, 1, D), lambda b, s: (row0 + b * row_stride, 0, col))

    args, in_specs, out_shape, out_specs = [x], [xspec], [], []
    if resid is not None:
        y, gpost, mods, gate_col = resid
        args += [y.reshape(B, S, D), gpost.reshape(1, D), mods]
        in_specs += [xspec, gspec, mspec(gate_col)]
        out_shape.append(jax.ShapeDtypeStruct((B, S, D), F32))
        out_specs.append(xspec)
    if norm is not None:
        gpre, mods, sc_col, sh_col = norm
        args += [gpre.reshape(1, D), mods, mods]
        in_specs += [gspec, mspec(sc_col), mspec(sh_col)]
        out_shape.append(jax.ShapeDtypeStruct((B, S, D), BF16))
        out_specs.append(xspec)
    outs = pl.pallas_call(
        functools.partial(_resid_norm_kernel, has_resid=resid is not None, has_norm=norm is not None),
        grid=(B, S // ts),
        in_specs=in_specs,
        out_specs=out_specs,
        out_shape=out_shape,
        compiler_params=_params("parallel", "parallel"),
        name="resid_norm",
    )(*args)
    return outs


def _mm_kernel(x_ref, w_ref, o_ref):
    o_ref[...] = jnp.dot(x_ref[...].astype(BF16), w_ref[...], preferred_element_type=F32).astype(o_ref.dtype)


def _matmul(x, w, *, out_dtype=F32, x_col_block=0, tm_pref=1024, tn_pref=512):
    M = x.shape[0]
    K, N = w.shape
    tm = _tile(M, tm_pref, BF16_ROWS)
    tn = _tile(N, tn_pref, LANES)
    return pl.pallas_call(
        _mm_kernel,
        grid=(M // tm, N // tn),
        in_specs=[
            pl.BlockSpec((tm, K), lambda i, j: (i, x_col_block)),
            pl.BlockSpec((K, tn), lambda i, j: (0, j)),
        ],
        out_specs=pl.BlockSpec((tm, tn), lambda i, j: (i, j)),
        out_shape=jax.ShapeDtypeStruct((M, N), out_dtype),
        compiler_params=_params("parallel", "parallel"),
        name="matmul",
    )(x, w)


def _rms_cols_kernel(x_ref, g_ref, *o_refs):
    y = _rms(x_ref[...], g_ref[...])
    for o_ref in o_refs:
        o_ref[...] = y.astype(o_ref.dtype)


def _rms_cols(x, col_block, g, out_dtypes):
    M = x.shape[0]
    W = g.shape[-1]
    tm = _tile(M, 1024, BF16_ROWS)
    spec = pl.BlockSpec((tm, W), lambda i: (i, 0))
    return pl.pallas_call(
        _rms_cols_kernel,
        grid=(M // tm,),
        in_specs=[pl.BlockSpec((tm, W), lambda i: (i, col_block)), pl.BlockSpec((1, W), lambda i: (0, 0))],
        out_specs=[spec] * len(out_dtypes),
        out_shape=[jax.ShapeDtypeStruct((M, W), dt) for dt in out_dtypes],
        compiler_params=_params("parallel"),
        name="rms_cols",
    )(x, g.reshape(1, W))


def _rope_tables(S, width):
    d = width // 2
    half = d // 2
    inv = ROPE_BASE ** (-jnp.arange(half, dtype=F32) / half)
    t = jnp.arange(S)
    row = (t // GRID_W).astype(F32)[:, None] * inv[None, :]
    col = (t % GRID_W).astype(F32)[:, None] * inv[None, :]
    cos = jnp.concatenate([jnp.cos(row), jnp.cos(row), jnp.cos(col), jnp.cos(col)], axis=-1)
    sin = jnp.concatenate([-jnp.sin(row), jnp.sin(row), -jnp.sin(col), jnp.sin(col)], axis=-1)
    reps = LANES // width
    return jnp.tile(cos, (1, reps)), jnp.tile(sin, (1, reps))


def _rope_kernel(x_ref, cos_ref, sin_ref, o_ref, *, pair):
    x = x_ref[0]
    lane = lax.broadcasted_iota(jnp.int32, x.shape, 1)
    first = (lane % (2 * pair)) < pair
    partner = jnp.where(first, pltpu.roll(x, LANES - pair, 1), pltpu.roll(x, pair, 1))
    o_ref[0] = (x * cos_ref[...] + partner * sin_ref[...]).astype(o_ref.dtype)


def _rope(x, col_block0, n_col_blocks, tables, pair):
    B, S, _ = x.shape
    cos, sin = tables
    ts = _tile(S, 512, BF16_ROWS)
    tspec = pl.BlockSpec((ts, LANES), lambda b, s, j: (s, 0))
    return pl.pallas_call(
        functools.partial(_rope_kernel, pair=pair),
        grid=(B, S // ts, n_col_blocks),
        in_specs=[pl.BlockSpec((1, ts, LANES), lambda b, s, j: (b, s, col_block0 + j)), tspec, tspec],
        out_specs=pl.BlockSpec((1, ts, LANES), lambda b, s, j: (b, s, j)),
        out_shape=jax.ShapeDtypeStruct((B, S, n_col_blocks * LANES), BF16),
        compiler_params=_params("parallel", "parallel", "parallel"),
        name="rope",
    )(x, cos, sin)


def _dot_nt(a, b):
    return lax.dot_general(a, b, (((1,), (1,)), ((), ())), preferred_element_type=F32)


def _gqa_kernel(*refs, G, hd, tq, S, windowed):
    if windowed:
        q_ref, k_ref, v_ref, sink_ref, ck_ref, cv_ref, o_ref = refs
    else:
        q_ref, k_ref, v_ref, sink_ref, o_ref = refs
    scale = hd ** -0.5
    if windowed:
        n = pl.program_id(2)
        span = 3 * Q_BLOCK
        start = pl.multiple_of(jnp.clip(n * tq - Q_BLOCK, 0, S - span), Q_BLOCK)
        kw = k_ref[0, pl.ds(start, span), :].astype(BF16)
        vw = v_ref[0, pl.ds(start, span), :].astype(BF16)
        qpos = n * tq + lax.broadcasted_iota(jnp.int32, (tq, span), 0)
        kpos = start + lax.broadcasted_iota(jnp.int32, (tq, span), 1)
        band = jnp.abs(kpos - qpos) <= WINDOW
        ck = ck_ref[0].astype(BF16)
        cv = cv_ref[0].astype(BF16)
    else:
        kw = k_ref[0].astype(BF16)
        vw = v_ref[0].astype(BF16)
    for g in range(G):
        q = q_ref[0, :, g * hd:(g + 1) * hd].astype(BF16)
        sink = sink_ref[0, g:g + 1, 0:1]
        s = _dot_nt(q, kw) * scale
        if windowed:
            s = jnp.where(band, s, NEG_INF)
        m = jnp.maximum(jnp.max(s, axis=-1, keepdims=True), sink)
        if windowed:
            sc = _dot_nt(q, ck) * scale
            m = jnp.maximum(m, jnp.max(sc, axis=-1, keepdims=True))
        p = jnp.exp(s - m)
        l = jnp.sum(p, axis=-1, keepdims=True) + jnp.exp(sink - m)
        acc = jnp.dot(p.astype(BF16), vw, preferred_element_type=F32)
        if windowed:
            pc = jnp.exp(sc - m)
            l = l + jnp.sum(pc, axis=-1, keepdims=True)
            acc = acc + jnp.dot(pc.astype(BF16), cv, preferred_element_type=F32)
        o_ref[0, :, g * hd:(g + 1) * hd] = (acc / l).astype(o_ref.dtype)


def _gqa(q, q_cb0, k, k_cb0, v, v_cb0, sink, ctx, *, KV, G, hd):
    B, S, _ = q.shape
    windowed = ctx is not None
    tq = Q_BLOCK if windowed else _tile(S, 256, BF16_ROWS)
    sink3 = jnp.broadcast_to(sink.astype(F32).reshape(KV, G, 1), (KV, G, LANES))
    in_specs = [
        pl.BlockSpec((1, tq, G * hd), lambda b, h, n: (b, n, q_cb0 + h)),
        pl.BlockSpec((1, S, hd), lambda b, h, n: (b, 0, k_cb0 + h)),
        pl.BlockSpec((1, S, hd), lambda b, h, n: (b, 0, v_cb0 + h)),
        pl.BlockSpec((1, G, LANES), lambda b, h, n: (h, 0, 0)),
    ]
    args = [q, k, v, sink3]
    if windowed:
        ck, cv = ctx
        P = ck.shape[1]
        cspec = pl.BlockSpec((1, P, hd), lambda b, h, n: (b, 0, h))
        in_specs += [cspec, cspec]
        args += [ck, cv]
    return pl.pallas_call(
        functools.partial(_gqa_kernel, G=G, hd=hd, tq=tq, S=S, windowed=windowed),
        grid=(B, KV, S // tq),
        in_specs=in_specs,
        out_specs=pl.BlockSpec((1, tq, G * hd), lambda b, h, n: (b, n, h)),
        out_shape=jax.ShapeDtypeStruct((B, S, KV * G * hd), BF16),
        compiler_params=_params("parallel", "parallel", "parallel"),
        name="gqa_attention",
    )(*args)


POOL_PAD = 16


def _pool_kernel(u_ref, w_ref, ps_ref, o_ref, pad_ref, *, S):
    grp = pl.program_id(1)
    u = u_ref[0]
    C = u.shape[-1]
    pad_ref[0:POOL_PAD, :] = jnp.zeros((POOL_PAD, C), F32)
    pad_ref[POOL_PAD + S:2 * POOL_PAD + S, :] = jnp.zeros((POOL_PAD, C), F32)
    pad_ref[POOL_PAD:POOL_PAD + S, :] = u
    t = lax.broadcasted_iota(jnp.int32, (S, 1), 0)
    for gi, win in enumerate(POOL_WINDOWS):
        @pl.when(grp == gi)
        def _(win=win):
            half = win // 2
            total = pad_ref[POOL_PAD - half:POOL_PAD - half + S, :]
            for off in range(-half + 1, half):
                total = total + pad_ref[POOL_PAD + off:POOL_PAD + off + S, :]
            count = (jnp.clip(t + half, 0, S) - jnp.clip(t - half, 0, S)).astype(F32)
            pooled = (total / count - u).astype(BF16)
            y = jnp.dot(pooled, w_ref[0].astype(BF16), preferred_element_type=F32) * ps_ref[...]
            o_ref[0] = y.astype(o_ref.dtype)


def _pool(u, u_cb0, w_pool, pool_scale):
    B, S, _ = u.shape
    NG, Cg, _ = w_pool.shape
    assert NG == len(POOL_WINDOWS)
    return pl.pallas_call(
        functools.partial(_pool_kernel, S=S),
        grid=(B, NG),
        in_specs=[
            pl.BlockSpec((1, S, Cg), lambda b, g: (b, 0, u_cb0 + g)),
            pl.BlockSpec((1, Cg, Cg), lambda b, g: (g, 0, 0)),
            pl.BlockSpec((1, Cg), lambda b, g: (0, g)),
        ],
        out_specs=pl.BlockSpec((1, S, Cg), lambda b, g: (b, 0, g)),
        out_shape=jax.ShapeDtypeStruct((B, S, NG * Cg), BF16),
        scratch_shapes=[pltpu.VMEM((S + 2 * POOL_PAD, Cg), F32)],
        compiler_params=_params("parallel", "parallel"),
        name="pool_mixer",
    )(u, w_pool, pool_scale.reshape(1, NG * Cg))


def _mla_kernel(qn_ref, qr_ref, kn_ref, kr_ref, v_ref, o_ref, *, dn, scale):
    qr = qr_ref[0].astype(BF16)
    for e in range(2):
        q = jnp.concatenate([qn_ref[0, :, e * dn:(e + 1) * dn].astype(BF16), qr], axis=-1)
        k = jnp.concatenate([kn_ref[0, :, e * dn:(e + 1) * dn], kr_ref[0, :, e * LANES:(e + 1) * LANES]], axis=-1)
        s = _dot_nt(q, k) * scale
        m = jnp.max(s, axis=-1, keepdims=True)
        p = jnp.exp(s - m)
        l = jnp.sum(p, axis=-1, keepdims=True)
        acc = jnp.dot(p.astype(BF16), v_ref[0, :, e * dn:(e + 1) * dn], preferred_element_type=F32)
        o_ref[0, :, e * dn:(e + 1) * dn] = (acc / l).astype(o_ref.dtype)


def _mla(qn, qn_cb0, qr, qr_cb0, knv, kr2, *, H, dn, dr):
    B, S, _ = qn.shape
    Sk = knv.shape[1]
    assert dn == LANES and 2 * dr == LANES
    tq = _tile(S, 256, BF16_ROWS)
    HP = H // 2
    return pl.pallas_call(
        functools.partial(_mla_kernel, dn=dn, scale=(dn + dr) ** -0.5),
        grid=(B, HP, S // tq),
        in_specs=[
            pl.BlockSpec((1, tq, 2 * dn), lambda b, h, n: (b, n, qn_cb0 + h)),
            pl.BlockSpec((1, tq, LANES), lambda b, h, n: (b, n, qr_cb0 + h)),
            pl.BlockSpec((1, Sk, 2 * dn), lambda b, h, n: (b, 0, h)),
            pl.BlockSpec((1, Sk, 2 * LANES), lambda b, h, n: (b, 0, 0)),
            pl.BlockSpec((1, Sk, 2 * dn), lambda b, h, n: (b, 0, HP + h)),
        ],
        out_specs=pl.BlockSpec((1, tq, 2 * dn), lambda b, h, n: (b, n, h)),
        out_shape=jax.ShapeDtypeStruct((B, S, H * dn), BF16),
        compiler_params=_params("parallel", "parallel", "parallel"),
        name="mla_attention",
    )(qn, qr, knv, kr2, knv)


def _diff_kernel(q_ref, k_ref, v_ref, l1_ref, l2_ref, gs_ref, o_ref, *, dk, lam_init):
    scale = dk ** -0.5
    lam = (jnp.exp(jnp.sum(l1_ref[0:1, :] * l1_ref[1:2, :], axis=-1, keepdims=True))
           - jnp.exp(jnp.sum(l2_ref[0:1, :] * l2_ref[1:2, :], axis=-1, keepdims=True)) + lam_init)
    ps = []
    for mi in range(2):
        q = q_ref[0, :, mi * dk:(mi + 1) * dk].astype(BF16)
        k = k_ref[0, :, mi * dk:(mi + 1) * dk].astype(BF16)
        s = _dot_nt(q, k) * scale
        e = jnp.exp(s - jnp.max(s, axis=-1, keepdims=True))
        ps.append(e / jnp.sum(e, axis=-1, keepdims=True))
    w = (ps[0] - lam * ps[1]).astype(BF16)
    out = jnp.dot(w, v_ref[0].astype(BF16), preferred_element_type=F32)
    o_ref[0] = (_rms(out, gs_ref[...]) * (1.0 - lam_init)).astype(o_ref.dtype)


def _diff(q, q_cb0, k, k_cb0, v, v_cb0, lam_q1, lam_k1, lam_q2, lam_k2, g_subln, lam_init, *, H, dk):
    B, S, _ = q.shape
    Sk = k.shape[1]
    dv = 2 * dk
    tq = _tile(S, 256, BF16_ROWS)
    l1 = jnp.stack([lam_q1, lam_k1]).astype(F32)
    l2 = jnp.stack([lam_q2, lam_k2]).astype(F32)
    lspec = pl.BlockSpec((2, dk), lambda b, h, n: (0, 0))
    return pl.pallas_call(
        functools.partial(_diff_kernel, dk=dk, lam_init=lam_init),
        grid=(B, H, S // tq),
        in_specs=[
            pl.BlockSpec((1, tq, dv), lambda b, h, n: (b, n, q_cb0 + h)),
            pl.BlockSpec((1, Sk, dv), lambda b, h, n: (b, 0, k_cb0 + h)),
            pl.BlockSpec((1, Sk, dv), lambda b, h, n: (b, 0, v_cb0 + h)),
            lspec, lspec,
            pl.BlockSpec((1, dv), lambda b, h, n: (0, 0)),
        ],
        out_specs=pl.BlockSpec((1, tq, dv), lambda b, h, n: (b, n, h)),
        out_shape=jax.ShapeDtypeStruct((B, S, H * dv), BF16),
        compiler_params=_params("parallel", "parallel", "parallel"),
        name="diff_attention",
    )(q, k, v, l1, l2, g_subln.reshape(1, dv))


HALO = BF16_ROWS


def _ffn_up_kernel(h_ref, hp_ref, hn_ref, wg_ref, wv_ref, cwg_ref, cwv_ref, cbg_ref, cbv_ref, o_ref, lhs_ref,
                   *, tm, S):
    i = pl.program_id(0)

    @pl.when(pl.program_id(1) == 0)
    def _():
        lhs_ref[0:HALO, :] = hp_ref[...]
        lhs_ref[HALO:HALO + tm, :] = h_ref[...]
        lhs_ref[HALO + tm:2 * HALO + tm, :] = hn_ref[...]

    lhs = lhs_ref[...]
    pos = (i * tm + lax.broadcasted_iota(jnp.int32, (tm, 1), 0)) % S
    has_prev = pos != 0
    has_next = pos != S - 1
    rows = tm + 2 * HALO

    def conv(w_ref, cw_ref, cb_ref):
        z = jnp.dot(lhs, w_ref[...], preferred_element_type=F32)
        zp = pltpu.roll(z, 1, 0)[HALO:HALO + tm]
        zn = pltpu.roll(z, rows - 1, 0)[HALO:HALO + tm]
        zc = z[HALO:HALO + tm]
        return (jnp.where(has_prev, zp, 0.0) * cw_ref[0:1, :] + zc * cw_ref[1:2, :]
                + jnp.where(has_next, zn, 0.0) * cw_ref[2:3, :] + cb_ref[...])

    gate = conv(wg_ref, cwg_ref, cbg_ref)
    val = conv(wv_ref, cwv_ref, cbv_ref)
    o_ref[...] = (gate * (1.0 / (1.0 + jnp.exp(-gate))) * val).astype(o_ref.dtype)


def _ffn_up(h, S, w_up, conv_w, conv_b):
    M, D = h.shape
    F = w_up.shape[1] // 2
    tm = _tile(M, 1024, S) if S <= 1024 else _tile(S, 1024, BF16_ROWS)
    tn = _tile(F, 256, LANES)
    nj = F // tn
    nhb = M // HALO
    tpb = tm // HALO
    cb = conv_b.reshape(1, 2 * F)
    return pl.pallas_call(
        functools.partial(_ffn_up_kernel, tm=tm, S=S),
        grid=(M // tm, nj),
        in_specs=[
            pl.BlockSpec((tm, D), lambda i, j: (i, 0)),
            pl.BlockSpec((HALO, D), lambda i, j: (jnp.maximum(i * tpb - 1, 0), 0)),
            pl.BlockSpec((HALO, D), lambda i, j: (jnp.minimum((i + 1) * tpb, nhb - 1), 0)),
            pl.BlockSpec((D, tn), lambda i, j: (0, j)),
            pl.BlockSpec((D, tn), lambda i, j: (0, nj + j)),
            pl.BlockSpec((3, tn), lambda i, j: (0, j)),
            pl.BlockSpec((3, tn), lambda i, j: (0, nj + j)),
            pl.BlockSpec((1, tn), lambda i, j: (0, j)),
            pl.BlockSpec((1, tn), lambda i, j: (0, nj + j)),
        ],
        out_specs=pl.BlockSpec((tm, tn), lambda i, j: (i, j)),
        out_shape=jax.ShapeDtypeStruct((M, F), BF16),
        scratch_shapes=[pltpu.VMEM((tm + 2 * HALO, D), BF16)],
        compiler_params=_params("parallel", "arbitrary"),
        name="ffn_up_conv_gate",
    )(h, h, h, w_up, w_up, conv_w, conv_w, cb, cb)


def kernel(x_prompt, x_sample, cache_a_k, cache_a_v, cache_c_ckv, cache_c_krope, cache_d_k, cache_d_v, c, c_ctx, w_mod, b_mod, g_pre_mix, g_post_mix, g_pre_ffn, g_post_ffn, w_in_even, w_out_even, a_sink, w_pool, pool_scale, w_in_odd, w_out_odd, g_q_norm, w_uq, g_kv_norm, w_uk, w_uv, lambda_q1, lambda_k1, lambda_q2, lambda_k2, g_subln, w_up, conv_w, conv_b, w_down):
    B1, S1, D = x_prompt.shape
    B2, S2, _ = x_sample.shape
    depth = w_mod.shape[0]
    P = cache_a_k.shape[2]
    KV, hd = cache_a_k.shape[3], cache_a_k.shape[4]
    AH = a_sink.shape[1]
    G = AH // KV
    NG, Cg = w_pool.shape[1], w_pool.shape[2]
    qa, kva, pool_dim = AH * hd, KV * hd, NG * Cg
    CQ = g_q_norm.shape[1]
    CKV, CH, CN = w_uk.shape[1], w_uk.shape[2], w_uk.shape[3]
    CR = cache_c_krope.shape[3]
    CV = w_uv.shape[3]
    DH, DK = cache_d_k.shape[3], cache_d_k.shape[5]
    DV = 2 * DK
    dqk = DH * 2 * DK
    assert hd == LANES and CN == LANES and CV == CN and 2 * CR == LANES and DK == LANES

    R = -(-(B2 + 1) // 8) * 8
    cond = jnp.zeros((R, D), F32).at[:B2].set(c).at[B2].set(c_ctx)
    mods = _mods(cond, w_mod, b_mod).reshape(depth, R, 1, 6 * D)
    SH_M, SC_M, G_M, SH_F, SC_F, G_F = range(6)

    groups = [
        dict(x=x_prompt, B=B1, S=S1, row0=B2, stride=0, latent=False),
        dict(x=x_sample, B=B2, S=S2, row0=0, stride=1, latent=True),
    ]
    rope_hd = _rope_tables(S2, hd)
    rope_cr = _rope_tables(S2, CR)

    new_state = {}
    for grp in groups:
        B, S, latent = grp["B"], grp["S"], grp["latent"]
        M = B * S
        row0, stride = grp["row0"], grp["stride"]
        x = grp["x"]
        (h,) = _resid_norm(x, row0, stride, norm=(g_pre_mix[0], mods[0], SC_M, SH_M))
        for i in range(depth):
            j = i // 2
            hf = h.reshape(M, D)
            if i % 2 == 0:
                w_in = w_in_even[j].astype(BF16)
                proj = _matmul(hf, w_in).reshape(B, S, -1)
                if latent:
                    qr_ = _rope(proj, 0, qa // LANES, rope_hd, hd // 4)
                    kr_ = _rope(proj, qa // LANES, kva // LANES, rope_hd, hd // 4)
                    ctx = (cache_a_k[:, j].reshape(B, P, kva), cache_a_v[:, j].reshape(B, P, kva))
                    a = _gqa(qr_, 0, kr_, 0, proj, (qa + kva) // hd, a_sink[j], ctx, KV=KV, G=G, hd=hd)
                else:
                    a = _gqa(proj, 0, proj, qa // hd, proj, (qa + kva) // hd, a_sink[j], None, KV=KV, G=G, hd=hd)
                    new_state["a_k"] = proj[:, :, qa:qa + kva].reshape(B, 1, S, KV, hd)
                    new_state["a_v"] = proj[:, :, qa + kva:qa + 2 * kva].reshape(B, 1, S, KV, hd)
                bmix = _pool(proj, (qa + 2 * kva) // Cg, w_pool[j], pool_scale[j])
                mix = jnp.concatenate([a, bmix], axis=-1).reshape(M, qa + pool_dim)
                y = _matmul(mix, w_out_even[j].astype(BF16))
            else:
                lam_init = 0.8 - 0.6 * math.exp(-0.3 * i)
                wi = w_in_odd[j]
                c0, c1, c2 = CQ, CQ + CKV, CQ + CKV + CR
                w_main = jnp.concatenate([wi[:, :c1], wi[:, c2:]], axis=1).astype(BF16)
                w_kr = jnp.concatenate([wi[:, c1:c2], jnp.zeros((D, LANES - CR), F32)], axis=1).astype(BF16)
                proj = _matmul(hf, w_main)
                krp = _matmul(hf, w_kr, tn_pref=LANES).reshape(B, S, LANES)
                o_dq, o_dk, o_dv = c1, c1 + dqk, c1 + 2 * dqk
                (cqn,) = _rms_cols(proj, 0, g_q_norm[j], [BF16])
                ckvn_f, ckvn_b = _rms_cols(proj, CQ // CKV, g_kv_norm[j], [F32, BF16])
                wq = w_uq[j].reshape(CQ, CH, CN + CR)
                wq = jnp.concatenate([wq[:, :, :CN].reshape(CQ, CH * CN), wq[:, :, CN:].reshape(CQ, CH * CR)], axis=1)
                qall = _matmul(cqn, wq.astype(BF16)).reshape(B, S, CH * (CN + CR))
                wkv = jnp.concatenate([w_uk[j].reshape(CKV, CH * CN), w_uv[j].reshape(CKV, CH * CV)], axis=1).astype(BF16)
                proj3 = proj.reshape(B, S, -1)
                if latent:
                    ckv_all = jnp.concatenate([ckvn_b.reshape(B, S, CKV), cache_c_ckv[:, j].astype(BF16)], axis=1)
                    kr_rot = _rope(krp, 0, 1, rope_cr, CR // 4)[:, :, :CR]
                    kr_all = jnp.concatenate([kr_rot, cache_c_krope[:, j].astype(BF16)], axis=1)
                    qrope = _rope(qall, CH * CN // LANES, CH * CR // LANES, rope_cr, CR // 4)
                    qr_arr, qr_cb0 = qrope, 0
                    dq_arr = _rope(proj3, o_dq // LANES, dqk // LANES, rope_hd, hd // 4)
                    dq_cb0 = 0
                    dk_rot = _rope(proj3, o_dk // LANES, dqk // LANES, rope_hd, hd // 4)
                    dk_arr = jnp.concatenate([dk_rot, cache_d_k[:, j].reshape(B, P, dqk).astype(BF16)], axis=1)
                    dk_cb0 = 0
                    dv_arr = jnp.concatenate([proj3[:, :, o_dv:o_dv + dqk].astype(BF16),
                                              cache_d_v[:, j].reshape(B, P, DH * DV).astype(BF16)], axis=1)
                    dv_cb0 = 0
                else:
                    ckv_all = ckvn_b.reshape(B, S, CKV)
                    kr_all = krp[:, :, :CR].astype(BF16)
                    qr_arr, qr_cb0 = qall, CH * CN // LANES
                    dq_arr, dq_cb0 = proj3, o_dq // DV
                    dk_arr, dk_cb0 = proj3, o_dk // DV
                    dv_arr, dv_cb0 = proj3, o_dv // DV
                    new_state["c_ckv"] = ckvn_f.reshape(B, 1, S, CKV)
                    new_state["c_krope"] = krp[:, :, :CR].reshape(B, 1, S, CR)
                    new_state["d_k"] = proj3[:, :, o_dk:o_dk + dqk].reshape(B, 1, S, DH, 2, DK)
                    new_state["d_v"] = proj3[:, :, o_dv:o_dv + dqk].reshape(B, 1, S, DH, DV)
                Sk = ckv_all.shape[1]
                knv = _matmul(ckv_all.reshape(B * Sk, CKV), wkv, out_dtype=BF16).reshape(B, Sk, -1)
                zk = jnp.zeros_like(kr_all)
                kr2 = jnp.concatenate([kr_all, zk, zk, kr_all], axis=-1)
                c_out = _mla(qall, 0, qr_arr, qr_cb0, knv, kr2, H=CH, dn=CN, dr=CR)
                d_out = _diff(dq_arr, dq_cb0, dk_arr, dk_cb0, dv_arr, dv_cb0,
                              lambda_q1[j], lambda_k1[j], lambda_q2[j], lambda_k2[j], g_subln[j], lam_init, H=DH, dk=DK)
                mix = jnp.concatenate([c_out, d_out], axis=-1).reshape(M, -1)
                y = _matmul(mix, w_out_odd[j].astype(BF16))
            x, h = _resid_norm(x, row0, stride, resid=(y, g_post_mix[i], mods[i], G_M),
                               norm=(g_pre_ffn[i], mods[i], SC_F, SH_F))
            hmid = _ffn_up(h.reshape(M, D), S, w_up[i].astype(BF16), conv_w[i], conv_b[i])
            f = _matmul(hmid, w_down[i].astype(BF16), tm_pref=512, tn_pref=512)
            if i + 1 < depth:
                x, h = _resid_norm(x, row0, stride, resid=(f, g_post_ffn[i], mods[i], G_F),
                                   norm=(g_pre_mix[i + 1], mods[i + 1], SC_M, SH_M))
            else:
                (x,) = _resid_norm(x, row0, stride, resid=(f, g_post_ffn[i], mods[i], G_F))
        grp["out"] = x

    return (groups[0]["out"], groups[1]["out"], new_state["a_k"], new_state["a_v"], new_state["c_ckv"],
            new_state["c_krope"], new_state["d_k"], new_state["d_v"])
```

```python
import functools
import math

import jax
import jax.numpy as jnp
from jax import lax
from jax.experimental import pallas as pl
from jax.experimental.pallas import tpu as pltpu

GRID_W = 64
Q_BLOCK = 128
WINDOW = 128
ROPE_BASE = 10000.0
EPS = 1e-6
NEG_INF = -1e30
POOL_WINDOWS = (2, 4, 8, 16)

LANES = 128
BF16_ROWS = 16
VMEM_LIMIT = 56 * 1024 * 1024

BF16 = jnp.bfloat16
F32 = jnp.float32


def _tile(dim, pref, align):
    best = None
    t = align
    while t <= min(dim, pref):
        if dim % t == 0:
            best = t
        t += align
    return dim if best is None else best


def _params(*sem):
    return pltpu.CompilerParams(dimension_semantics=sem, vmem_limit_bytes=VMEM_LIMIT)


def _mods_kernel(c_ref, w_ref, b_ref, o_ref):
    c = c_ref[...]
    s = (c * (1.0 / (1.0 + jnp.exp(-c)))).astype(BF16)
    o_ref[...] = jnp.dot(s, w_ref[...].astype(BF16), preferred_element_type=F32) + b_ref[...]


def _mods(cond, w_mod, b_mod):
    depth, D, N = w_mod.shape
    R = cond.shape[0]
    tn = _tile(N, 512, LANES)
    return pl.pallas_call(
        _mods_kernel,
        grid=(depth, N // tn),
        in_specs=[
            pl.BlockSpec((R, D), lambda l, j: (0, 0)),
            pl.BlockSpec((None, D, tn), lambda l, j: (l, 0, j)),
            pl.BlockSpec((None, 1, tn), lambda l, j: (l, 0, j)),
        ],
        out_specs=pl.BlockSpec((None, R, tn), lambda l, j: (l, 0, j)),
        out_shape=jax.ShapeDtypeStruct((depth, R, N), F32),
        compiler_params=_params("parallel", "parallel"),
        name="mods",
    )(cond, w_mod, b_mod.reshape(depth, 1, N))


def _rms(x, g):
    return x * lax.rsqrt(jnp.mean(x * x, axis=-1, keepdims=True) + EPS) * g


def _resid_norm_kernel(*refs, has_resid, has_norm):
    refs = list(refs)
    x = refs.pop(0)[0]
    if has_resid:
        y_ref, gpost_ref, gate_ref = refs[:3]
        refs = refs[3:]
        x = x + gate_ref[...] * _rms(y_ref[0], gpost_ref[...])
    if has_norm:
        gpre_ref, sc_ref, sh_ref = refs[:3]
        refs = refs[3:]
    if has_resid:
        refs.pop(0)[0] = x
    if has_norm:
        h = _rms(x, gpre_ref[...]) * (1.0 + sc_ref[...]) + sh_ref[...]
        refs.pop(0)[0] = h.astype(BF16)


def _resid_norm(x, row0, row_stride, *, resid=None, norm=None):
    B, S, D = x.shape
    ts = _tile(S, 256, 8)
    xspec = pl.BlockSpec((1, ts, D), lambda b, s: (b, s, 0))
    gspec = pl.BlockSpec((1, D), lambda b, s: (0, 0))

    def mspec(col):
        return pl.BlockSpec((None, 1, D), lambda b, s: (row0 + b * row_stride, 0, col))

    args, in_specs, out_shape, out_specs = [x], [xspec], [], []
    if resid is not None:
        y, gpost, mods, gate_col = resid
        args += [y.reshape(B, S, D), gpost.reshape(1, D), mods]
        in_specs += [xspec, gspec, mspec(gate_col)]
        out_shape.append(jax.ShapeDtypeStruct((B, S, D), F32))
        out_specs.append(xspec)
    if norm is not None:
        gpre, mods, sc_col, sh_col = norm
        args += [gpre.reshape(1, D), mods, mods]
        in_specs += [gspec, mspec(sc_col), mspec(sh_col)]
        out_shape.append(jax.ShapeDtypeStruct((B, S, D), BF16))
        out_specs.append(xspec)
    outs = pl.pallas_call(
        functools.partial(_resid_norm_kernel, has_resid=resid is not None, has_norm=norm is not None),
        grid=(B, S // ts),
        in_specs=in_specs,
        out_specs=out_specs,
        out_shape=out_shape,
        compiler_params=_params("parallel", "parallel"),
        name="resid_norm",
    )(*args)
    return outs


def _mm_kernel(x_ref, w_ref, o_ref):
    o_ref[...] = jnp.dot(x_ref[...].astype(BF16), w_ref[...], preferred_element_type=F32).astype(o_ref.dtype)


def _rotate_slab(x, cos, sin, first, pair):
    partner = jnp.where(first, pltpu.roll(x, LANES - pair, 1), pltpu.roll(x, pair, 1))
    return x * cos + partner * sin


def _mm_rope_kernel(x_ref, w_ref, cos_ref, sin_ref, o_ref, *, pair, g0, g1):
    tm, tn = o_ref.shape
    spt = tn // LANES
    acc = jnp.dot(x_ref[...].astype(BF16), w_ref[...], preferred_element_type=F32)
    j = pl.program_id(1)
    lane = lax.broadcasted_iota(jnp.int32, (tm, LANES), 1)
    first = (lane % (2 * pair)) < pair

    def emit(rotated):
        for c in range(spt):
            x = acc[:, c * LANES:(c + 1) * LANES]
            if rotated[c]:
                x = _rotate_slab(x, cos_ref[...], sin_ref[...], first, pair)
            o_ref[:, c * LANES:(c + 1) * LANES] = x.astype(o_ref.dtype)

    full_lo, full_hi = -(-g0 // spt), g1 // spt
    mixed = [t for t in {g0 // spt, (g1 - 1) // spt} if not full_lo <= t < full_hi]
    is_full = (j >= full_lo) & (j < full_hi)
    is_mixed = functools.reduce(jnp.logical_or, [j == t for t in mixed], jnp.bool_(False))
    pl.when(is_full)(lambda: emit([True] * spt))
    for t in mixed:
        pl.when(j == t)(lambda t=t: emit([g0 <= t * spt + c < g1 for c in range(spt)]))
    pl.when(jnp.logical_not(is_full | is_mixed))(lambda: emit([False] * spt))


def _matmul(x, w, *, out_dtype=F32, tm_pref=1024, tn_pref=512, rope=None):
    M, K = x.shape
    N = w.shape[1]
    tn = _tile(N, tn_pref, LANES)
    if rope is None:
        tm = _tile(M, tm_pref, BF16_ROWS)
        kern, extra_specs, extra_args = _mm_kernel, [], []
    else:
        (cos, sin), pair, g0, g1, S = rope
        tm = _tile(S, tm_pref, BF16_ROWS)
        tps = S // tm
        tspec = pl.BlockSpec((tm, LANES), lambda i, j: (i % tps, 0))
        kern = functools.partial(_mm_rope_kernel, pair=pair, g0=g0, g1=g1)
        extra_specs, extra_args = [tspec, tspec], [cos, sin]
    return pl.pallas_call(
        kern,
        grid=(M // tm, N // tn),
        in_specs=[
            pl.BlockSpec((tm, K), lambda i, j: (i, 0)),
            pl.BlockSpec((K, tn), lambda i, j: (0, j)),
        ] + extra_specs,
        out_specs=pl.BlockSpec((tm, tn), lambda i, j: (i, j)),
        out_shape=jax.ShapeDtypeStruct((M, N), out_dtype),
        compiler_params=_params("parallel", "parallel"),
        name="matmul",
    )(x, w, *extra_args)


def _rms_cols_kernel(x_ref, g_ref, *o_refs):
    y = _rms(x_ref[...].astype(F32), g_ref[...])
    for o_ref in o_refs:
        o_ref[...] = y.astype(o_ref.dtype)


def _rms_cols(x, col_block, g, out_dtypes):
    M = x.shape[0]
    W = g.shape[-1]
    tm = _tile(M, 1024, BF16_ROWS)
    spec = pl.BlockSpec((tm, W), lambda i: (i, 0))
    return pl.pallas_call(
        _rms_cols_kernel,
        grid=(M // tm,),
        in_specs=[pl.BlockSpec((tm, W), lambda i: (i, col_block)), pl.BlockSpec((1, W), lambda i: (0, 0))],
        out_specs=[spec] * len(out_dtypes),
        out_shape=[jax.ShapeDtypeStruct((M, W), dt) for dt in out_dtypes],
        compiler_params=_params("parallel"),
        name="rms_cols",
    )(x, g.reshape(1, W))


def _rope_tables(S, width):
    d = width // 2
    half = d // 2
    inv = ROPE_BASE ** (-jnp.arange(half, dtype=F32) / half)
    t = jnp.arange(S)
    row = (t // GRID_W).astype(F32)[:, None] * inv[None, :]
    col = (t % GRID_W).astype(F32)[:, None] * inv[None, :]
    cos = jnp.concatenate([jnp.cos(row), jnp.cos(row), jnp.cos(col), jnp.cos(col)], axis=-1)
    sin = jnp.concatenate([-jnp.sin(row), jnp.sin(row), -jnp.sin(col), jnp.sin(col)], axis=-1)
    reps = LANES // width
    return jnp.tile(cos, (1, reps)), jnp.tile(sin, (1, reps))


LOG2E = math.log2(math.e)


def _dot_nt(a, b):
    return lax.dot_general(a, b, (((1,), (1,)), ((), ())), preferred_element_type=F32)


def _gqa_kernel(*refs, G, hd, tq, S, windowed):
    if windowed:
        q_ref, k_ref, v_ref, sink_ref, ck_ref, cv_ref, o_ref = refs
    else:
        q_ref, k_ref, v_ref, sink_ref, o_ref = refs
    c = hd ** -0.5 * LOG2E
    rows = G * tq
    q = jnp.concatenate([q_ref[0, :, g * hd:(g + 1) * hd] for g in range(G)], axis=0).astype(BF16)
    sink = jnp.concatenate([jnp.broadcast_to(sink_ref[0, g:g + 1, 0:1], (tq, 1)) for g in range(G)], axis=0) * LOG2E
    if windowed:
        n = pl.program_id(2)
        span = 3 * Q_BLOCK
        start = pl.multiple_of(jnp.clip(n * tq - Q_BLOCK, 0, S - span), Q_BLOCK)
        kw = k_ref[0, pl.ds(start, span), :].astype(BF16)
        vw = v_ref[0, pl.ds(start, span), :].astype(BF16)
        qpos = n * tq + lax.broadcasted_iota(jnp.int32, (rows, span), 0) % tq
        kpos = start + lax.broadcasted_iota(jnp.int32, (rows, span), 1)
        s = jnp.where(jnp.abs(kpos - qpos) <= WINDOW, _dot_nt(q, kw) * c, NEG_INF)
        sc = _dot_nt(q, ck_ref[0].astype(BF16)) * c
        m = jnp.maximum(jnp.maximum(jnp.max(s, axis=-1, keepdims=True), jnp.max(sc, axis=-1, keepdims=True)), sink)
    else:
        kw = k_ref[0].astype(BF16)
        vw = v_ref[0].astype(BF16)
        s = _dot_nt(q, kw) * c
        m = jnp.maximum(jnp.max(s, axis=-1, keepdims=True), sink)
    p = jnp.exp2(s - m)
    l = jnp.sum(p, axis=-1, keepdims=True) + jnp.exp2(sink - m)
    acc = jnp.dot(p.astype(BF16), vw, preferred_element_type=F32)
    if windowed:
        pc = jnp.exp2(sc - m)
        l = l + jnp.sum(pc, axis=-1, keepdims=True)
        acc = acc + jnp.dot(pc.astype(BF16), cv_ref[0].astype(BF16), preferred_element_type=F32)
    out = acc / l
    for g in range(G):
        o_ref[0, :, g * hd:(g + 1) * hd] = out[g * tq:(g + 1) * tq].astype(o_ref.dtype)


def _gqa(q, q_cb0, k, k_cb0, v, v_cb0, sink, ctx, *, KV, G, hd):
    B, S, _ = q.shape
    windowed = ctx is not None
    tq = Q_BLOCK if windowed else _tile(S, 256, BF16_ROWS)
    sink3 = jnp.broadcast_to(sink.astype(F32).reshape(KV, G, 1), (KV, G, LANES))
    in_specs = [
        pl.BlockSpec((1, tq, G * hd), lambda b, h, n: (b, n, q_cb0 + h)),
        pl.BlockSpec((1, S, hd), lambda b, h, n: (b, 0, k_cb0 + h)),
        pl.BlockSpec((1, S, hd), lambda b, h, n: (b, 0, v_cb0 + h)),
        pl.BlockSpec((1, G, LANES), lambda b, h, n: (h, 0, 0)),
    ]
    args = [q, k, v, sink3]
    if windowed:
        ck, cv = ctx
        P = ck.shape[1]
        cspec = pl.BlockSpec((1, P, hd), lambda b, h, n: (b, 0, h))
        in_specs += [cspec, cspec]
        args += [ck, cv]
    return pl.pallas_call(
        functools.partial(_gqa_kernel, G=G, hd=hd, tq=tq, S=S, windowed=windowed),
        grid=(B, KV, S // tq),
        in_specs=in_specs,
        out_specs=pl.BlockSpec((1, tq, G * hd), lambda b, h, n: (b, n, h)),
        out_shape=jax.ShapeDtypeStruct((B, S, KV * G * hd), BF16),
        compiler_params=_params("parallel", "parallel", "parallel"),
        name="gqa_attention",
    )(*args)


POOL_PAD = 16


def _pool_kernel(u_ref, w_ref, ps_ref, o_ref, pad_ref, *, S):
    grp = pl.program_id(1)
    u = u_ref[0].astype(F32)
    C = u.shape[-1]
    pad_ref[0:POOL_PAD, :] = jnp.zeros((POOL_PAD, C), F32)
    pad_ref[POOL_PAD + S:2 * POOL_PAD + S, :] = jnp.zeros((POOL_PAD, C), F32)
    pad_ref[POOL_PAD:POOL_PAD + S, :] = u
    t = lax.broadcasted_iota(jnp.int32, (S, 1), 0)
    for gi, win in enumerate(POOL_WINDOWS):
        @pl.when(grp == gi)
        def _(win=win):
            half = win // 2
            total = pad_ref[POOL_PAD - half:POOL_PAD - half + S, :]
            for off in range(-half + 1, half):
                total = total + pad_ref[POOL_PAD + off:POOL_PAD + off + S, :]
            count = (jnp.clip(t + half, 0, S) - jnp.clip(t - half, 0, S)).astype(F32)
            pooled = (total / count - u).astype(BF16)
            y = jnp.dot(pooled, w_ref[0].astype(BF16), preferred_element_type=F32) * ps_ref[...]
            o_ref[0] = y.astype(o_ref.dtype)


def _pool(u, u_cb0, w_pool, pool_scale):
    B, S, _ = u.shape
    NG, Cg, _ = w_pool.shape
    assert NG == len(POOL_WINDOWS)
    return pl.pallas_call(
        functools.partial(_pool_kernel, S=S),
        grid=(B, NG),
        in_specs=[
            pl.BlockSpec((1, S, Cg), lambda b, g: (b, 0, u_cb0 + g)),
            pl.BlockSpec((1, Cg, Cg), lambda b, g: (g, 0, 0)),
            pl.BlockSpec((1, Cg), lambda b, g: (0, g)),
        ],
        out_specs=pl.BlockSpec((1, S, Cg), lambda b, g: (b, 0, g)),
        out_shape=jax.ShapeDtypeStruct((B, S, NG * Cg), BF16),
        scratch_shapes=[pltpu.VMEM((S + 2 * POOL_PAD, Cg), F32)],
        compiler_params=_params("parallel", "parallel"),
        name="pool_mixer",
    )(u, w_pool, pool_scale.reshape(1, NG * Cg))


def _mla_kernel(qn_ref, qr_ref, kn_ref, kr_ref, v_ref, o_ref, *, dn, scale):
    qr = qr_ref[0].astype(BF16)
    for e in range(2):
        q = jnp.concatenate([qn_ref[0, :, e * dn:(e + 1) * dn].astype(BF16), qr], axis=-1)
        k = jnp.concatenate([kn_ref[0, :, e * dn:(e + 1) * dn], kr_ref[0, :, e * LANES:(e + 1) * LANES]], axis=-1)
        s = _dot_nt(q, k) * (scale * LOG2E)
        m = jnp.max(s, axis=-1, keepdims=True)
        p = jnp.exp2(s - m)
        l = jnp.sum(p, axis=-1, keepdims=True)
        acc = jnp.dot(p.astype(BF16), v_ref[0, :, e * dn:(e + 1) * dn], preferred_element_type=F32)
        o_ref[0, :, e * dn:(e + 1) * dn] = (acc / l).astype(o_ref.dtype)


def _mla(qn, qn_cb0, qr, qr_cb0, knv, kr2, *, H, dn, dr):
    B, S, _ = qn.shape
    Sk = knv.shape[1]
    assert dn == LANES and 2 * dr == LANES
    tq = _tile(S, 256, BF16_ROWS)
    HP = H // 2
    return pl.pallas_call(
        functools.partial(_mla_kernel, dn=dn, scale=(dn + dr) ** -0.5),
        grid=(B, HP, S // tq),
        in_specs=[
            pl.BlockSpec((1, tq, 2 * dn), lambda b, h, n: (b, n, qn_cb0 + h)),
            pl.BlockSpec((1, tq, LANES), lambda b, h, n: (b, n, qr_cb0 + h)),
            pl.BlockSpec((1, Sk, 2 * dn), lambda b, h, n: (b, 0, h)),
            pl.BlockSpec((1, Sk, 2 * LANES), lambda b, h, n: (b, 0, 0)),
            pl.BlockSpec((1, Sk, 2 * dn), lambda b, h, n: (b, 0, HP + h)),
        ],
        out_specs=pl.BlockSpec((1, tq, 2 * dn), lambda b, h, n: (b, n, h)),
        out_shape=jax.ShapeDtypeStruct((B, S, H * dn), BF16),
        compiler_params=_params("parallel", "parallel", "parallel"),
        name="mla_attention",
    )(qn, qr, knv, kr2, knv)


def _diff_kernel(q_ref, k_ref, v_ref, l1_ref, l2_ref, gs_ref, o_ref, *, dk, lam_init):
    c = dk ** -0.5 * LOG2E
    lam = (jnp.exp(jnp.sum(l1_ref[0:1, :] * l1_ref[1:2, :], axis=-1, keepdims=True))
           - jnp.exp(jnp.sum(l2_ref[0:1, :] * l2_ref[1:2, :], axis=-1, keepdims=True)) + lam_init)
    es, rs = [], []
    for mi in range(2):
        q = q_ref[0, :, mi * dk:(mi + 1) * dk].astype(BF16)
        k = k_ref[0, :, mi * dk:(mi + 1) * dk].astype(BF16)
        s = _dot_nt(q, k) * c
        e = jnp.exp2(s - jnp.max(s, axis=-1, keepdims=True))
        es.append(e)
        rs.append(1.0 / jnp.sum(e, axis=-1, keepdims=True))
    w = (es[0] * rs[0] - es[1] * (lam * rs[1])).astype(BF16)
    out = jnp.dot(w, v_ref[0].astype(BF16), preferred_element_type=F32)
    o_ref[0] = (_rms(out, gs_ref[...]) * (1.0 - lam_init)).astype(o_ref.dtype)


def _diff(q, q_cb0, k, k_cb0, v, v_cb0, lam_q1, lam_k1, lam_q2, lam_k2, g_subln, lam_init, *, H, dk):
    B, S, _ = q.shape
    Sk = k.shape[1]
    dv = 2 * dk
    tq = _tile(S, 256, BF16_ROWS)
    l1 = jnp.stack([lam_q1, lam_k1]).astype(F32)
    l2 = jnp.stack([lam_q2, lam_k2]).astype(F32)
    lspec = pl.BlockSpec((2, dk), lambda b, h, n: (0, 0))
    return pl.pallas_call(
        functools.partial(_diff_kernel, dk=dk, lam_init=lam_init),
        grid=(B, H, S // tq),
        in_specs=[
            pl.BlockSpec((1, tq, dv), lambda b, h, n: (b, n, q_cb0 + h)),
            pl.BlockSpec((1, Sk, dv), lambda b, h, n: (b, 0, k_cb0 + h)),
            pl.BlockSpec((1, Sk, dv), lambda b, h, n: (b, 0, v_cb0 + h)),
            lspec, lspec,
            pl.BlockSpec((1, dv), lambda b, h, n: (0, 0)),
        ],
        out_specs=pl.BlockSpec((1, tq, dv), lambda b, h, n: (b, n, h)),
        out_shape=jax.ShapeDtypeStruct((B, S, H * dv), BF16),
        compiler_params=_params("parallel", "parallel", "parallel"),
        name="diff_attention",
    )(q, k, v, l1, l2, g_subln.reshape(1, dv))


Z_PAD = 8
FFN_ROW_CHUNK = 32
FFN_DOT_ROWS = 1024


def _ffn_up_kernel(h_ref, wg_ref, wv_ref, cwg_ref, cwv_ref, cbg_ref, cbv_ref, o_ref, z0_ref, z1_ref,
                   *, tm, tn, S, nj, rc, rm):
    j = pl.program_id(1)
    zs = (z0_ref, z1_ref)

    def project(z_ref):
        zero = jnp.zeros((Z_PAD, 2 * tn), F32)
        z_ref[0:Z_PAD, :] = zero
        z_ref[Z_PAD + tm:2 * Z_PAD + tm, :] = zero
        for r0 in range(0, tm, rm):
            h = h_ref[r0:r0 + rm, :]
            z_ref[Z_PAD + r0:Z_PAD + r0 + rm, 0:tn] = jnp.dot(h, wg_ref[...], preferred_element_type=F32)
            z_ref[Z_PAD + r0:Z_PAD + r0 + rm, tn:2 * tn] = jnp.dot(h, wv_ref[...], preferred_element_type=F32)

    def gate(z_ref):
        cw = jnp.concatenate([cwg_ref[...], cwv_ref[...]], axis=1)
        cb = jnp.concatenate([cbg_ref[...], cbv_ref[...]], axis=1)
        span = rc + 2 * Z_PAD
        for ch in range(tm // rc):
            r0 = ch * rc
            zf = z_ref[r0:r0 + span, :]
            zc = zf[Z_PAD:Z_PAD + rc]
            zp = pltpu.roll(zf, 1, 0)[Z_PAD:Z_PAD + rc]
            zn = pltpu.roll(zf, span - 1, 0)[Z_PAD:Z_PAD + rc]
            if tm > S:
                pos = (r0 + lax.broadcasted_iota(jnp.int32, (rc, 1), 0)) % S
                zp = jnp.where(pos != 0, zp, 0.0)
                zn = jnp.where(pos != S - 1, zn, 0.0)
            y = zp * cw[0:1, :] + zc * cw[1:2, :] + zn * cw[2:3, :] + cb
            g = y[:, 0:tn]
            o_ref[r0:r0 + rc, :] = (g * (1.0 / (1.0 + jnp.exp(-g))) * y[:, tn:2 * tn]).astype(o_ref.dtype)

    pl.when(j == 0)(lambda: project(zs[0]))
    for par in range(2):
        @pl.when((j > 0) & (j < nj) & (j % 2 == par))
        def _(par=par):
            project(zs[par])
            gate(zs[1 - par])
    pl.when(j == nj)(lambda: gate(zs[(nj - 1) % 2]))


def _ffn_up(h, S, w_up, conv_w, conv_b):
    M, D = h.shape
    F = w_up.shape[1] // 2
    tm = _tile(M, max(2048, S), S)
    assert tm % S == 0
    tn = _tile(F, 256, LANES)
    nj = F // tn
    rc = _tile(tm, FFN_ROW_CHUNK, 8)
    rm = _tile(tm, FFN_DOT_ROWS, BF16_ROWS)
    cb = conv_b.reshape(1, 2 * F)
    cur = lambda i, j: (0, jnp.minimum(j, nj - 1))
    cur_v = lambda i, j: (0, nj + jnp.minimum(j, nj - 1))
    prev = lambda i, j: (0, jnp.maximum(j - 1, 0))
    prev_v = lambda i, j: (0, nj + jnp.maximum(j - 1, 0))
    return pl.pallas_call(
        functools.partial(_ffn_up_kernel, tm=tm, tn=tn, S=S, nj=nj, rc=rc, rm=rm),
        grid=(M // tm, nj + 1),
        in_specs=[
            pl.BlockSpec((tm, D), lambda i, j: (i, 0), pipeline_mode=pl.Buffered(1)),
            pl.BlockSpec((D, tn), cur),
            pl.BlockSpec((D, tn), cur_v),
            pl.BlockSpec((3, tn), prev),
            pl.BlockSpec((3, tn), prev_v),
            pl.BlockSpec((1, tn), prev),
            pl.BlockSpec((1, tn), prev_v),
        ],
        out_specs=pl.BlockSpec((tm, tn), lambda i, j: (i, jnp.maximum(j - 1, 0))),
        out_shape=jax.ShapeDtypeStruct((M, F), BF16),
        scratch_shapes=[pltpu.VMEM((tm + 2 * Z_PAD, 2 * tn), F32)] * 2,
        compiler_params=_params("parallel", "arbitrary"),
        name="ffn_up_conv_gate",
    )(h, w_up, w_up, conv_w, conv_w, cb, cb)


def kernel(x_prompt, x_sample, cache_a_k, cache_a_v, cache_c_ckv, cache_c_krope, cache_d_k, cache_d_v, c, c_ctx, w_mod, b_mod, g_pre_mix, g_post_mix, g_pre_ffn, g_post_ffn, w_in_even, w_out_even, a_sink, w_pool, pool_scale, w_in_odd, w_out_odd, g_q_norm, w_uq, g_kv_norm, w_uk, w_uv, lambda_q1, lambda_k1, lambda_q2, lambda_k2, g_subln, w_up, conv_w, conv_b, w_down):
    B1, S1, D = x_prompt.shape
    B2, S2, _ = x_sample.shape
    depth = w_mod.shape[0]
    P = cache_a_k.shape[2]
    KV, hd = cache_a_k.shape[3], cache_a_k.shape[4]
    AH = a_sink.shape[1]
    G = AH // KV
    NG, Cg = w_pool.shape[1], w_pool.shape[2]
    qa, kva, pool_dim = AH * hd, KV * hd, NG * Cg
    CQ = g_q_norm.shape[1]
    CKV, CH, CN = w_uk.shape[1], w_uk.shape[2], w_uk.shape[3]
    CR = cache_c_krope.shape[3]
    CV = w_uv.shape[3]
    DH, DK = cache_d_k.shape[3], cache_d_k.shape[5]
    DV = 2 * DK
    dqk = DH * 2 * DK
    assert hd == LANES and CN == LANES and CV == CN and 2 * CR == LANES and DK == LANES

    R = -(-(B2 + 1) // 8) * 8
    cond = jnp.zeros((R, D), F32).at[:B2].set(c).at[B2].set(c_ctx)
    mods = _mods(cond, w_mod, b_mod).reshape(depth, R, 1, 6 * D)
    SH_M, SC_M, G_M, SH_F, SC_F, G_F = range(6)

    groups = [
        dict(x=x_prompt, B=B1, S=S1, row0=B2, stride=0, latent=False),
        dict(x=x_sample, B=B2, S=S2, row0=0, stride=1, latent=True),
    ]
    rope_hd = _rope_tables(S2, hd)
    rope_cr = _rope_tables(S2, CR)

    new_state = {}
    for grp in groups:
        B, S, latent = grp["B"], grp["S"], grp["latent"]
        M = B * S
        row0, stride = grp["row0"], grp["stride"]
        x = grp["x"]
        (h,) = _resid_norm(x, row0, stride, norm=(g_pre_mix[0], mods[0], SC_M, SH_M))
        for i in range(depth):
            j = i // 2
            hf = h.reshape(M, D)
            pdt = BF16 if latent else F32

            def rope(tables, pair, col0, col1):
                return (tables, pair, col0 // LANES, col1 // LANES, S) if latent else None

            if i % 2 == 0:
                w_in = w_in_even[j].astype(BF16)
                proj = _matmul(hf, w_in, out_dtype=pdt, rope=rope(rope_hd, hd // 4, 0, qa + kva)).reshape(B, S, -1)
                if latent:
                    ctx = (cache_a_k[:, j].reshape(B, P, kva), cache_a_v[:, j].reshape(B, P, kva))
                else:
                    ctx = None
                    new_state["a_k"] = proj[:, :, qa:qa + kva].reshape(B, 1, S, KV, hd)
                    new_state["a_v"] = proj[:, :, qa + kva:qa + 2 * kva].reshape(B, 1, S, KV, hd)
                a = _gqa(proj, 0, proj, qa // hd, proj, (qa + kva) // hd, a_sink[j], ctx, KV=KV, G=G, hd=hd)
                bmix = _pool(proj, (qa + 2 * kva) // Cg, w_pool[j], pool_scale[j])
                mix = jnp.concatenate([a, bmix], axis=-1).reshape(M, qa + pool_dim)
                y = _matmul(mix, w_out_even[j].astype(BF16))
            else:
                lam_init = 0.8 - 0.6 * math.exp(-0.3 * i)
                wi = w_in_odd[j]
                c1, c2 = CQ + CKV, CQ + CKV + CR
                w_main = jnp.concatenate([wi[:, :c1], wi[:, c2:]], axis=1).astype(BF16)
                w_kr = jnp.concatenate([wi[:, c1:c2], jnp.zeros((D, LANES - CR), F32)], axis=1).astype(BF16)
                o_dq, o_dk, o_dv = c1, c1 + dqk, c1 + 2 * dqk
                proj = _matmul(hf, w_main, out_dtype=pdt, rope=rope(rope_hd, hd // 4, o_dq, o_dv))
                krp = _matmul(hf, w_kr, out_dtype=pdt, tn_pref=LANES,
                              rope=rope(rope_cr, CR // 4, 0, LANES)).reshape(B, S, LANES)
                (cqn,) = _rms_cols(proj, 0, g_q_norm[j], [BF16])
                ckvn_f, ckvn_b = _rms_cols(proj, CQ // CKV, g_kv_norm[j], [F32, BF16])
                wq = w_uq[j].reshape(CQ, CH, CN + CR)
                wq = jnp.concatenate([wq[:, :, :CN].reshape(CQ, CH * CN), wq[:, :, CN:].reshape(CQ, CH * CR)], axis=1)
                qall = _matmul(cqn, wq.astype(BF16), out_dtype=pdt,
                               rope=rope(rope_cr, CR // 4, CH * CN, CH * (CN + CR))).reshape(B, S, CH * (CN + CR))
                wkv = jnp.concatenate([w_uk[j].reshape(CKV, CH * CN), w_uv[j].reshape(CKV, CH * CV)], axis=1).astype(BF16)
                proj3 = proj.reshape(B, S, -1)
                if latent:
                    ckv_all = jnp.concatenate([ckvn_b.reshape(B, S, CKV), cache_c_ckv[:, j].astype(BF16)], axis=1)
                    kr_all = jnp.concatenate([krp[:, :, :CR], cache_c_krope[:, j].astype(BF16)], axis=1)
                    dk_arr = jnp.concatenate([proj3[:, :, o_dk:o_dk + dqk],
                                              cache_d_k[:, j].reshape(B, P, dqk).astype(BF16)], axis=1)
                    dv_arr = jnp.concatenate([proj3[:, :, o_dv:o_dv + dqk],
                                              cache_d_v[:, j].reshape(B, P, DH * DV).astype(BF16)], axis=1)
                    dk_cb0 = dv_cb0 = 0
                else:
                    ckv_all = ckvn_b.reshape(B, S, CKV)
                    kr_all = krp[:, :, :CR].astype(BF16)
                    dk_arr, dk_cb0 = proj3, o_dk // DV
                    dv_arr, dv_cb0 = proj3, o_dv // DV
                    new_state["c_ckv"] = ckvn_f.reshape(B, 1, S, CKV)
                    new_state["c_krope"] = krp[:, :, :CR].reshape(B, 1, S, CR)
                    new_state["d_k"] = proj3[:, :, o_dk:o_dk + dqk].reshape(B, 1, S, DH, 2, DK)
                    new_state["d_v"] = proj3[:, :, o_dv:o_dv + dqk].reshape(B, 1, S, DH, DV)
                Sk = ckv_all.shape[1]
                knv = _matmul(ckv_all.reshape(B * Sk, CKV), wkv, out_dtype=BF16).reshape(B, Sk, -1)
                zk = jnp.zeros_like(kr_all)
                kr2 = jnp.concatenate([kr_all, zk, zk, kr_all], axis=-1)
                c_out = _mla(qall, 0, qall, CH * CN // LANES, knv, kr2, H=CH, dn=CN, dr=CR)
                d_out = _diff(proj3, o_dq // DV, dk_arr, dk_cb0, dv_arr, dv_cb0,
                              lambda_q1[j], lambda_k1[j], lambda_q2[j], lambda_k2[j], g_subln[j], lam_init, H=DH, dk=DK)
                mix = jnp.concatenate([c_out, d_out], axis=-1).reshape(M, -1)
                y = _matmul(mix, w_out_odd[j].astype(BF16))
            x, h = _resid_norm(x, row0, stride, resid=(y, g_post_mix[i], mods[i], G_M),
                               norm=(g_pre_ffn[i], mods[i], SC_F, SH_F))
            hmid = _ffn_up(h.reshape(M, D), S, w_up[i].astype(BF16), conv_w[i], conv_b[i])
            f = _matmul(hmid, w_down[i].astype(BF16), tm_pref=512, tn_pref=512)
            if i + 1 < depth:
                x, h = _resid_norm(x, row0, stride, resid=(f, g_post_ffn[i], mods[i], G_F),
                                   norm=(g_pre_mix[i + 1], mods[i + 1], SC_M, SH_M))
            else:
                (x,) = _resid_norm(x, row0, stride, resid=(f, g_post_ffn[i], mods[i], G_F))
        grp["out"] = x

    return (groups[0]["out"], groups[1]["out"], new_state["a_k"], new_state["a_v"], new_state["c_ckv"],
            new_state["c_krope"], new_state["d_k"], new_state["d_v"])
```

```python
import functools
import math

import jax
import jax.numpy as jnp
from jax import lax
from jax.experimental import pallas as pl
from jax.experimental.pallas import tpu as pltpu

GRID_W = 64
Q_BLOCK = 128
WINDOW = 128
ROPE_BASE = 10000.0
EPS = 1e-6
NEG_INF = -1e30
POOL_WINDOWS = (2, 4, 8, 16)

LANES = 128
BF16_ROWS = 16
VMEM_LIMIT = 56 * 1024 * 1024

BF16 = jnp.bfloat16
F32 = jnp.float32


def _tile(dim, pref, align):
    best = None
    t = align
    while t <= min(dim, pref):
        if dim % t == 0:
            best = t
        t += align
    return dim if best is None else best


def _params(*sem):
    return pltpu.CompilerParams(dimension_semantics=sem, vmem_limit_bytes=VMEM_LIMIT)


def _mods_kernel(c_ref, w_ref, b_ref, o_ref):
    c = c_ref[...]
    s = (c * (1.0 / (1.0 + jnp.exp(-c)))).astype(BF16)
    o_ref[...] = jnp.dot(s, w_ref[...].astype(BF16), preferred_element_type=F32) + b_ref[...]


def _mods(cond, w_mod, b_mod):
    depth, D, N = w_mod.shape
    R = cond.shape[0]
    tn = _tile(N, 512, LANES)
    return pl.pallas_call(
        _mods_kernel,
        grid=(depth, N // tn),
        in_specs=[
            pl.BlockSpec((R, D), lambda l, j: (0, 0)),
            pl.BlockSpec((None, D, tn), lambda l, j: (l, 0, j)),
            pl.BlockSpec((None, 1, tn), lambda l, j: (l, 0, j)),
        ],
        out_specs=pl.BlockSpec((None, R, tn), lambda l, j: (l, 0, j)),
        out_shape=jax.ShapeDtypeStruct((depth, R, N), F32),
        compiler_params=_params("parallel", "parallel"),
        name="mods",
    )(cond, w_mod, b_mod.reshape(depth, 1, N))


def _rms(x, g):
    return x * lax.rsqrt(jnp.mean(x * x, axis=-1, keepdims=True) + EPS) * g


def _resid_norm_kernel(*refs, has_resid, has_norm):
    refs = list(refs)
    x = refs.pop(0)[0]
    if has_resid:
        y_ref, gpost_ref, gate_ref = refs[:3]
        refs = refs[3:]
        x = x + gate_ref[...] * _rms(y_ref[0], gpost_ref[...])
    if has_norm:
        gpre_ref, sc_ref, sh_ref = refs[:3]
        refs = refs[3:]
    if has_resid:
        refs.pop(0)[0] = x
    if has_norm:
        h = _rms(x, gpre_ref[...]) * (1.0 + sc_ref[...]) + sh_ref[...]
        refs.pop(0)[0] = h.astype(BF16)


def _resid_norm(x, row0, row_stride, *, resid=None, norm=None):
    B, S, D = x.shape
    ts = _tile(S, 256, 8)
    xspec = pl.BlockSpec((1, ts, D), lambda b, s: (b, s, 0))
    gspec = pl.BlockSpec((1, D), lambda b, s: (0, 0))

    def mspec(col):
        return pl.BlockSpec((None, 1, D), lambda b, s: (row0 + b * row_stride, 0, col))

    args, in_specs, out_shape, out_specs = [x], [xspec], [], []
    if resid is not None:
        y, gpost, mods, gate_col = resid
        args += [y.reshape(B, S, D), gpost.reshape(1, D), mods]
        in_specs += [xspec, gspec, mspec(gate_col)]
        out_shape.append(jax.ShapeDtypeStruct((B, S, D), F32))
        out_specs.append(xspec)
    if norm is not None:
        gpre, mods, sc_col, sh_col = norm
        args += [gpre.reshape(1, D), mods, mods]
        in_specs += [gspec, mspec(sc_col), mspec(sh_col)]
        out_shape.append(jax.ShapeDtypeStruct((B, S, D), BF16))
        out_specs.append(xspec)
    outs = pl.pallas_call(
        functools.partial(_resid_norm_kernel, has_resid=resid is not None, has_norm=norm is not None),
        grid=(B, S // ts),
        in_specs=in_specs,
        out_specs=out_specs,
        out_shape=out_shape,
        compiler_params=_params("parallel", "parallel"),
        name="resid_norm",
    )(*args)
    return outs


def _mm_kernel(x_ref, w_ref, o_ref):
    o_ref[...] = jnp.dot(x_ref[...].astype(BF16), w_ref[...], preferred_element_type=F32).astype(o_ref.dtype)


def _mm_parts_kernel(*refs):
    o_ref = refs[-1]
    acc = None
    for x_ref, w_ref in zip(refs[0:-1:2], refs[1:-1:2]):
        d = jnp.dot(x_ref[...], w_ref[...], preferred_element_type=F32)
        acc = d if acc is None else acc + d
    o_ref[...] = acc.astype(o_ref.dtype)


def _matmul_parts(xs, w, *, out_dtype=F32, tm_pref=1024, tn_pref=512):
    M = xs[0].shape[0]
    N = w.shape[1]
    starts = [sum(x.shape[1] for x in xs[:n]) for n in range(len(xs))]
    if any(k0 % x.shape[1] for k0, x in zip(starts, xs)):
        return _matmul(jnp.concatenate(xs, axis=1), w, out_dtype=out_dtype, tm_pref=tm_pref, tn_pref=tn_pref)
    tm = _tile(M, tm_pref, BF16_ROWS)
    tn = _tile(N, tn_pref, LANES)
    in_specs, args, k0 = [], [], 0
    for x in xs:
        kp = x.shape[1]
        in_specs += [pl.BlockSpec((tm, kp), lambda i, j: (i, 0)),
                     pl.BlockSpec((kp, tn), lambda i, j, kb=k0 // kp: (kb, j))]
        args += [x, w]
        k0 += kp
    return pl.pallas_call(
        _mm_parts_kernel,
        grid=(M // tm, N // tn),
        in_specs=in_specs,
        out_specs=pl.BlockSpec((tm, tn), lambda i, j: (i, j)),
        out_shape=jax.ShapeDtypeStruct((M, N), out_dtype),
        compiler_params=_params("parallel", "parallel"),
        name="matmul_parts",
    )(*args)


def _rotate_slab(x, cos, sin, first, pair):
    partner = jnp.where(first, pltpu.roll(x, LANES - pair, 1), pltpu.roll(x, pair, 1))
    return x * cos + partner * sin


def _mm_rope_kernel(x_ref, w_ref, cos_ref, sin_ref, o_ref, *, pair, g0, g1):
    tm, tn = o_ref.shape
    spt = tn // LANES
    acc = jnp.dot(x_ref[...].astype(BF16), w_ref[...], preferred_element_type=F32)
    j = pl.program_id(1)
    lane = lax.broadcasted_iota(jnp.int32, (tm, LANES), 1)
    first = (lane % (2 * pair)) < pair

    def emit(rotated):
        for c in range(spt):
            x = acc[:, c * LANES:(c + 1) * LANES]
            if rotated[c]:
                x = _rotate_slab(x, cos_ref[...], sin_ref[...], first, pair)
            o_ref[:, c * LANES:(c + 1) * LANES] = x.astype(o_ref.dtype)

    full_lo, full_hi = -(-g0 // spt), g1 // spt
    mixed = [t for t in {g0 // spt, (g1 - 1) // spt} if not full_lo <= t < full_hi]
    is_full = (j >= full_lo) & (j < full_hi)
    is_mixed = functools.reduce(jnp.logical_or, [j == t for t in mixed], jnp.bool_(False))
    pl.when(is_full)(lambda: emit([True] * spt))
    for t in mixed:
        pl.when(j == t)(lambda t=t: emit([g0 <= t * spt + c < g1 for c in range(spt)]))
    pl.when(jnp.logical_not(is_full | is_mixed))(lambda: emit([False] * spt))


def _matmul(x, w, *, out_dtype=F32, tm_pref=1024, tn_pref=512, rope=None):
    M, K = x.shape
    N = w.shape[1]
    tn = _tile(N, tn_pref, LANES)
    if rope is None:
        tm = _tile(M, tm_pref, BF16_ROWS)
        kern, extra_specs, extra_args = _mm_kernel, [], []
    else:
        (cos, sin), pair, g0, g1, S = rope
        tm = _tile(S, tm_pref, BF16_ROWS)
        tps = S // tm
        tspec = pl.BlockSpec((tm, LANES), lambda i, j: (i % tps, 0))
        kern = functools.partial(_mm_rope_kernel, pair=pair, g0=g0, g1=g1)
        extra_specs, extra_args = [tspec, tspec], [cos, sin]
    return pl.pallas_call(
        kern,
        grid=(M // tm, N // tn),
        in_specs=[
            pl.BlockSpec((tm, K), lambda i, j: (i, 0)),
            pl.BlockSpec((K, tn), lambda i, j: (0, j)),
        ] + extra_specs,
        out_specs=pl.BlockSpec((tm, tn), lambda i, j: (i, j)),
        out_shape=jax.ShapeDtypeStruct((M, N), out_dtype),
        compiler_params=_params("parallel", "parallel"),
        name="matmul",
    )(x, w, *extra_args)


def _rms_cols_kernel(x_ref, g_ref, *o_refs):
    y = _rms(x_ref[...].astype(F32), g_ref[...])
    for o_ref in o_refs:
        o_ref[...] = y.astype(o_ref.dtype)


def _rms_cols(x, col_block, g, out_dtypes):
    M = x.shape[0]
    W = g.shape[-1]
    tm = _tile(M, 1024, BF16_ROWS)
    spec = pl.BlockSpec((tm, W), lambda i: (i, 0))
    return pl.pallas_call(
        _rms_cols_kernel,
        grid=(M // tm,),
        in_specs=[pl.BlockSpec((tm, W), lambda i: (i, col_block)), pl.BlockSpec((1, W), lambda i: (0, 0))],
        out_specs=[spec] * len(out_dtypes),
        out_shape=[jax.ShapeDtypeStruct((M, W), dt) for dt in out_dtypes],
        compiler_params=_params("parallel"),
        name="rms_cols",
    )(x, g.reshape(1, W))


def _rope_tables(S, width):
    d = width // 2
    half = d // 2
    inv = ROPE_BASE ** (-jnp.arange(half, dtype=F32) / half)
    t = jnp.arange(S)
    row = (t // GRID_W).astype(F32)[:, None] * inv[None, :]
    col = (t % GRID_W).astype(F32)[:, None] * inv[None, :]
    cos = jnp.concatenate([jnp.cos(row), jnp.cos(row), jnp.cos(col), jnp.cos(col)], axis=-1)
    sin = jnp.concatenate([-jnp.sin(row), jnp.sin(row), -jnp.sin(col), jnp.sin(col)], axis=-1)
    reps = LANES // width
    return jnp.tile(cos, (1, reps)), jnp.tile(sin, (1, reps))


LOG2E = math.log2(math.e)


def _logit_scale(d):
    return d ** -0.5 * LOG2E


def _dot_nt(a, b):
    return lax.dot_general(a, b, (((1,), (1,)), ((), ())), preferred_element_type=F32)


def _gqa_kernel(*refs, G, hd, tq, S, windowed):
    if windowed:
        q_ref, k_ref, v_ref, sink_ref, ck_ref, cv_ref, o_ref = refs
    else:
        q_ref, k_ref, v_ref, sink_ref, o_ref = refs
    rows = G * tq
    q = jnp.concatenate([q_ref[0, :, g * hd:(g + 1) * hd] for g in range(G)], axis=0).astype(BF16)
    sink = jnp.concatenate([jnp.broadcast_to(sink_ref[0, g:g + 1, :], (tq, LANES)) for g in range(G)], axis=0) * LOG2E
    if windowed:
        n = pl.program_id(2)
        span = 3 * Q_BLOCK
        start = pl.multiple_of(jnp.clip(n * tq - Q_BLOCK, 0, S - span), Q_BLOCK)
        kw = k_ref[0, pl.ds(start, span), :].astype(BF16)
        vw = v_ref[0, pl.ds(start, span), :].astype(BF16)
        qpos = n * tq + lax.broadcasted_iota(jnp.int32, (rows, span), 0) % tq
        kpos = start + lax.broadcasted_iota(jnp.int32, (rows, span), 1)
        s = jnp.where(jnp.abs(kpos - qpos) <= WINDOW, _dot_nt(q, kw), NEG_INF)
        sc = _dot_nt(q, ck_ref[0].astype(BF16))
        m = jnp.maximum(jnp.max(s, axis=-1, keepdims=True), jnp.max(sc, axis=-1, keepdims=True))
    else:
        kw = k_ref[0].astype(BF16)
        vw = v_ref[0].astype(BF16)
        s = _dot_nt(q, kw)
        m = jnp.max(s, axis=-1, keepdims=True)
    p = jnp.exp2(s - m)
    l = jnp.sum(p, axis=-1, keepdims=True)
    acc = jnp.dot(p.astype(BF16), vw, preferred_element_type=F32)
    if windowed:
        pc = jnp.exp2(sc - m)
        l = l + jnp.sum(pc, axis=-1, keepdims=True)
        acc = acc + jnp.dot(pc.astype(BF16), cv_ref[0].astype(BF16), preferred_element_type=F32)
    out = acc / (l + jnp.exp2(sink - m))
    for g in range(G):
        o_ref[0, :, g * hd:(g + 1) * hd] = out[g * tq:(g + 1) * tq].astype(o_ref.dtype)


def _gqa(q, q_cb0, k, k_cb0, v, v_cb0, sink, ctx, *, KV, G, hd):
    B, S, _ = q.shape
    windowed = ctx is not None
    tq = Q_BLOCK if windowed else _tile(S, 256, BF16_ROWS)
    sink3 = jnp.broadcast_to(sink.astype(F32).reshape(KV, G, 1), (KV, G, LANES))
    in_specs = [
        pl.BlockSpec((1, tq, G * hd), lambda b, h, n: (b, n, q_cb0 + h)),
        pl.BlockSpec((1, S, hd), lambda b, h, n: (b, 0, k_cb0 + h)),
        pl.BlockSpec((1, S, hd), lambda b, h, n: (b, 0, v_cb0 + h)),
        pl.BlockSpec((1, G, LANES), lambda b, h, n: (h, 0, 0)),
    ]
    args = [q, k, v, sink3]
    if windowed:
        ck, cv = ctx
        P = ck.shape[1]
        cspec = pl.BlockSpec((1, P, hd), lambda b, h, n: (b, 0, h))
        in_specs += [cspec, cspec]
        args += [ck, cv]
    return pl.pallas_call(
        functools.partial(_gqa_kernel, G=G, hd=hd, tq=tq, S=S, windowed=windowed),
        grid=(B, KV, S // tq),
        in_specs=in_specs,
        out_specs=pl.BlockSpec((1, tq, G * hd), lambda b, h, n: (b, n, h)),
        out_shape=jax.ShapeDtypeStruct((B, S, KV * G * hd), BF16),
        compiler_params=_params("parallel", "parallel", "parallel"),
        name="gqa_attention",
    )(*args)


POOL_PAD = 16


def _pool_kernel(u_ref, w_ref, ps_ref, o_ref, pad_ref, *, S):
    grp = pl.program_id(1)
    u = u_ref[0].astype(F32)
    C = u.shape[-1]
    pad_ref[0:POOL_PAD, :] = jnp.zeros((POOL_PAD, C), F32)
    pad_ref[POOL_PAD + S:2 * POOL_PAD + S, :] = jnp.zeros((POOL_PAD, C), F32)
    pad_ref[POOL_PAD:POOL_PAD + S, :] = u
    t = lax.broadcasted_iota(jnp.int32, (S, 1), 0)
    for gi, win in enumerate(POOL_WINDOWS):
        @pl.when(grp == gi)
        def _(win=win):
            half = win // 2
            total = pad_ref[POOL_PAD - half:POOL_PAD - half + S, :]
            for off in range(-half + 1, half):
                total = total + pad_ref[POOL_PAD + off:POOL_PAD + off + S, :]
            count = (jnp.clip(t + half, 0, S) - jnp.clip(t - half, 0, S)).astype(F32)
            pooled = (total / count - u).astype(BF16)
            y = jnp.dot(pooled, w_ref[0].astype(BF16), preferred_element_type=F32) * ps_ref[...]
            o_ref[0] = y.astype(o_ref.dtype)


def _pool(u, u_cb0, w_pool, pool_scale):
    B, S, _ = u.shape
    NG, Cg, _ = w_pool.shape
    assert NG == len(POOL_WINDOWS)
    return pl.pallas_call(
        functools.partial(_pool_kernel, S=S),
        grid=(B, NG),
        in_specs=[
            pl.BlockSpec((1, S, Cg), lambda b, g: (b, 0, u_cb0 + g)),
            pl.BlockSpec((1, Cg, Cg), lambda b, g: (g, 0, 0)),
            pl.BlockSpec((1, Cg), lambda b, g: (0, g)),
        ],
        out_specs=pl.BlockSpec((1, S, Cg), lambda b, g: (b, 0, g)),
        out_shape=jax.ShapeDtypeStruct((B, S, NG * Cg), BF16),
        scratch_shapes=[pltpu.VMEM((S + 2 * POOL_PAD, Cg), F32)],
        compiler_params=_params("parallel", "parallel"),
        name="pool_mixer",
    )(u, w_pool, pool_scale.reshape(1, NG * Cg))


def _mla_kernel(qn_ref, qr_ref, kn_ref, kr_ref, v_ref, o_ref, *, dn):
    qr = qr_ref[0].astype(BF16)
    for e in range(2):
        q = jnp.concatenate([qn_ref[0, :, e * dn:(e + 1) * dn].astype(BF16), qr], axis=-1)
        k = jnp.concatenate([kn_ref[0, :, e * dn:(e + 1) * dn], kr_ref[0, :, e * LANES:(e + 1) * LANES]], axis=-1)
        s = _dot_nt(q, k)
        m = jnp.max(s, axis=-1, keepdims=True)
        p = jnp.exp2(s - m)
        l = jnp.sum(p, axis=-1, keepdims=True)
        acc = jnp.dot(p.astype(BF16), v_ref[0, :, e * dn:(e + 1) * dn], preferred_element_type=F32)
        o_ref[0, :, e * dn:(e + 1) * dn] = (acc / l).astype(o_ref.dtype)


def _mla(qn, qn_cb0, qr, qr_cb0, knv, kr2, *, H, dn, dr):
    B, S, _ = qn.shape
    Sk = knv.shape[1]
    assert dn == LANES and 2 * dr == LANES
    tq = _tile(S, 256, BF16_ROWS)
    HP = H // 2
    return pl.pallas_call(
        functools.partial(_mla_kernel, dn=dn),
        grid=(B, HP, S // tq),
        in_specs=[
            pl.BlockSpec((1, tq, 2 * dn), lambda b, h, n: (b, n, qn_cb0 + h)),
            pl.BlockSpec((1, tq, LANES), lambda b, h, n: (b, n, qr_cb0 + h)),
            pl.BlockSpec((1, Sk, 2 * dn), lambda b, h, n: (b, 0, h)),
            pl.BlockSpec((1, Sk, 2 * LANES), lambda b, h, n: (b, 0, 0)),
            pl.BlockSpec((1, Sk, 2 * dn), lambda b, h, n: (b, 0, HP + h)),
        ],
        out_specs=pl.BlockSpec((1, tq, 2 * dn), lambda b, h, n: (b, n, h)),
        out_shape=jax.ShapeDtypeStruct((B, S, H * dn), BF16),
        compiler_params=_params("parallel", "parallel", "parallel"),
        name="mla_attention",
    )(qn, qr, knv, kr2, knv)


def _diff_kernel(*refs, dk, lam_init, has_ctx):
    if has_ctx:
        q_ref, k_ref, v_ref, ck_ref, cv_ref, l1_ref, l2_ref, gs_ref, o_ref = refs
        key_refs, val_refs = (k_ref, ck_ref), (v_ref, cv_ref)
    else:
        q_ref, k_ref, v_ref, l1_ref, l2_ref, gs_ref, o_ref = refs
        key_refs, val_refs = (k_ref,), (v_ref,)
    lam = (jnp.exp(jnp.sum(l1_ref[0:1, :] * l1_ref[1:2, :], axis=-1, keepdims=True))
           - jnp.exp(jnp.sum(l2_ref[0:1, :] * l2_ref[1:2, :], axis=-1, keepdims=True)) + lam_init)
    es, fs = [], []
    for mi in range(2):
        q = q_ref[0, :, mi * dk:(mi + 1) * dk].astype(BF16)
        ss = [_dot_nt(q, kr[0, :, mi * dk:(mi + 1) * dk].astype(BF16)) for kr in key_refs]
        m = functools.reduce(jnp.maximum, [jnp.max(s, axis=-1, keepdims=True) for s in ss])
        e = [jnp.exp2(s - m) for s in ss]
        tot = functools.reduce(jnp.add, [jnp.sum(x, axis=-1, keepdims=True) for x in e])
        es.append(e)
        fs.append((1.0 if mi == 0 else lam) / tot)
    out = None
    for si, vr in enumerate(val_refs):
        w = (es[0][si] * fs[0] - es[1][si] * fs[1]).astype(BF16)
        o = jnp.dot(w, vr[0].astype(BF16), preferred_element_type=F32)
        out = o if out is None else out + o
    o_ref[0] = (_rms(out, gs_ref[...]) * (1.0 - lam_init)).astype(o_ref.dtype)


def _diff(q, q_cb0, k, k_cb0, v, v_cb0, ctx, lam_q1, lam_k1, lam_q2, lam_k2, g_subln, lam_init, *, H, dk):
    B, S, _ = q.shape
    dv = 2 * dk
    tq = _tile(S, 256, BF16_ROWS)
    l1 = jnp.stack([lam_q1, lam_k1]).astype(F32)
    l2 = jnp.stack([lam_q2, lam_k2]).astype(F32)
    lspec = pl.BlockSpec((2, dk), lambda b, h, n: (0, 0))
    in_specs = [
        pl.BlockSpec((1, tq, dv), lambda b, h, n: (b, n, q_cb0 + h)),
        pl.BlockSpec((1, S, dv), lambda b, h, n: (b, 0, k_cb0 + h)),
        pl.BlockSpec((1, S, dv), lambda b, h, n: (b, 0, v_cb0 + h)),
    ]
    args = [q, k, v]
    if ctx is not None:
        P = ctx[0].shape[1]
        cspec = pl.BlockSpec((1, P, dv), lambda b, h, n: (b, 0, h))
        in_specs += [cspec, cspec]
        args += list(ctx)
    in_specs += [lspec, lspec, pl.BlockSpec((1, dv), lambda b, h, n: (0, 0))]
    args += [l1, l2, g_subln.reshape(1, dv)]
    return pl.pallas_call(
        functools.partial(_diff_kernel, dk=dk, lam_init=lam_init, has_ctx=ctx is not None),
        grid=(B, H, S // tq),
        in_specs=in_specs,
        out_specs=pl.BlockSpec((1, tq, dv), lambda b, h, n: (b, n, h)),
        out_shape=jax.ShapeDtypeStruct((B, S, H * dv), BF16),
        compiler_params=_params("parallel", "parallel", "parallel"),
        name="diff_attention",
    )(*args)


Z_PAD = 8
FFN_ROW_CHUNK = 32
FFN_DOT_ROWS = 1024


def _ffn_up_kernel(h_ref, wg_ref, wv_ref, cwg_ref, cwv_ref, cbg_ref, cbv_ref, o_ref, z0_ref, z1_ref, lhs_ref,
                   *, tm, tn, S, nj, rc, rm):
    j = pl.program_id(1)
    zs = (z0_ref, z1_ref)
    segments = [(r0, w_ref, c0) for r0 in range(0, tm, rm) for w_ref, c0 in ((wg_ref, 0), (wv_ref, tn))]
    chunks = tm // rc

    def project_segment(z_ref, seg, after=None):
        r0, w_ref, c0 = segments[seg]
        if after is not None:
            first = lhs_ref[r0:r0 + BF16_ROWS, 0:LANES]
            lhs_ref[r0:r0 + BF16_ROWS, 0:LANES] = first + after
        if seg == 0:
            zero = jnp.zeros((Z_PAD, 2 * tn), F32)
            z_ref[0:Z_PAD, :] = zero
            z_ref[Z_PAD + tm:2 * Z_PAD + tm, :] = zero
        z_ref[Z_PAD + r0:Z_PAD + r0 + rm, c0:c0 + tn] = jnp.dot(
            lhs_ref[r0:r0 + rm, :], w_ref[...], preferred_element_type=F32)

    def gate_chunks(z_ref, lo, hi):
        cw = jnp.concatenate([cwg_ref[...], cwv_ref[...]], axis=1)
        cb = jnp.concatenate([cbg_ref[...], cbv_ref[...]], axis=1)
        span = rc + 2 * Z_PAD
        for ch in range(lo, hi):
            r0 = ch * rc
            zf = z_ref[r0:r0 + span, :]
            zc = zf[Z_PAD:Z_PAD + rc]
            zp = pltpu.roll(zf, 1, 0)[Z_PAD:Z_PAD + rc]
            zn = pltpu.roll(zf, span - 1, 0)[Z_PAD:Z_PAD + rc]
            if tm > S:
                pos = (r0 + lax.broadcasted_iota(jnp.int32, (rc, 1), 0)) % S
                zp = jnp.where(pos != 0, zp, 0.0)
                zn = jnp.where(pos != S - 1, zn, 0.0)
            y = zp * cw[0:1, :] + zc * cw[1:2, :] + zn * cw[2:3, :] + cb
            g = y[:, 0:tn]
            out = (g * (1.0 / (1.0 + jnp.exp(-g))) * y[:, tn:2 * tn]).astype(o_ref.dtype)
            o_ref[r0:r0 + rc, :] = out
        return jnp.minimum(jnp.abs(out[0:BF16_ROWS, 0:LANES]), 0)

    @pl.when(j == 0)
    def _():
        lhs_ref[...] = h_ref[...]
        for seg in range(len(segments)):
            project_segment(zs[0], seg)

    for par in range(2):
        @pl.when((j > 0) & (j < nj) & (j % 2 == par))
        def _(par=par):
            parts = len(segments)
            token = None
            for seg in range(parts):
                project_segment(zs[par], seg, after=token)
                token = gate_chunks(zs[1 - par], seg * chunks // parts, (seg + 1) * chunks // parts)

    @pl.when(j == nj)
    def _():
        gate_chunks(zs[(nj - 1) % 2], 0, chunks)


def _ffn_up(h, S, w_up, conv_w, conv_b):
    M, D = h.shape
    F = w_up.shape[1] // 2
    tm = _tile(M, max(2048, S), S)
    assert tm % S == 0
    tn = _tile(F, 256, LANES)
    nj = F // tn
    rc = _tile(tm, FFN_ROW_CHUNK, 8)
    rm = _tile(tm, FFN_DOT_ROWS, BF16_ROWS)
    cb = conv_b.reshape(1, 2 * F)
    cur = lambda i, j: (0, jnp.minimum(j, nj - 1))
    cur_v = lambda i, j: (0, nj + jnp.minimum(j, nj - 1))
    prev = lambda i, j: (0, jnp.maximum(j - 1, 0))
    prev_v = lambda i, j: (0, nj + jnp.maximum(j - 1, 0))
    return pl.pallas_call(
        functools.partial(_ffn_up_kernel, tm=tm, tn=tn, S=S, nj=nj, rc=rc, rm=rm),
        grid=(M // tm, nj + 1),
        in_specs=[
            pl.BlockSpec((tm, D), lambda i, j: (i, 0), pipeline_mode=pl.Buffered(1)),
            pl.BlockSpec((D, tn), cur),
            pl.BlockSpec((D, tn), cur_v),
            pl.BlockSpec((3, tn), prev),
            pl.BlockSpec((3, tn), prev_v),
            pl.BlockSpec((1, tn), prev),
            pl.BlockSpec((1, tn), prev_v),
        ],
        out_specs=pl.BlockSpec((tm, tn), lambda i, j: (i, jnp.maximum(j - 1, 0))),
        out_shape=jax.ShapeDtypeStruct((M, F), BF16),
        scratch_shapes=[pltpu.VMEM((tm + 2 * Z_PAD, 2 * tn), F32)] * 2 + [pltpu.VMEM((tm, D), BF16)],
        compiler_params=_params("parallel", "arbitrary"),
        name="ffn_up_conv_gate",
    )(h, w_up, w_up, conv_w, conv_w, cb, cb)


def kernel(x_prompt, x_sample, cache_a_k, cache_a_v, cache_c_ckv, cache_c_krope, cache_d_k, cache_d_v, c, c_ctx, w_mod, b_mod, g_pre_mix, g_post_mix, g_pre_ffn, g_post_ffn, w_in_even, w_out_even, a_sink, w_pool, pool_scale, w_in_odd, w_out_odd, g_q_norm, w_uq, g_kv_norm, w_uk, w_uv, lambda_q1, lambda_k1, lambda_q2, lambda_k2, g_subln, w_up, conv_w, conv_b, w_down):
    B1, S1, D = x_prompt.shape
    B2, S2, _ = x_sample.shape
    depth = w_mod.shape[0]
    P = cache_a_k.shape[2]
    KV, hd = cache_a_k.shape[3], cache_a_k.shape[4]
    AH = a_sink.shape[1]
    G = AH // KV
    NG, Cg = w_pool.shape[1], w_pool.shape[2]
    qa, kva, pool_dim = AH * hd, KV * hd, NG * Cg
    CQ = g_q_norm.shape[1]
    CKV, CH, CN = w_uk.shape[1], w_uk.shape[2], w_uk.shape[3]
    CR = cache_c_krope.shape[3]
    CV = w_uv.shape[3]
    DH, DK = cache_d_k.shape[3], cache_d_k.shape[5]
    DV = 2 * DK
    dqk = DH * 2 * DK
    assert hd == LANES and CN == LANES and CV == CN and 2 * CR == LANES and DK == LANES

    R = -(-(B2 + 1) // 8) * 8
    cond = jnp.zeros((R, D), F32).at[:B2].set(c).at[B2].set(c_ctx)
    mods = _mods(cond, w_mod, b_mod).reshape(depth, R, 1, 6 * D)
    SH_M, SC_M, G_M, SH_F, SC_F, G_F = range(6)

    groups = [
        dict(x=x_prompt, B=B1, S=S1, row0=B2, stride=0, latent=False),
        dict(x=x_sample, B=B2, S=S2, row0=0, stride=1, latent=True),
    ]
    rope_hd = _rope_tables(S2, hd)
    rope_cr = _rope_tables(S2, CR)

    new_state = {}
    for grp in groups:
        B, S, latent = grp["B"], grp["S"], grp["latent"]
        M = B * S
        row0, stride = grp["row0"], grp["stride"]
        x = grp["x"]
        (h,) = _resid_norm(x, row0, stride, norm=(g_pre_mix[0], mods[0], SC_M, SH_M))
        for i in range(depth):
            j = i // 2
            hf = h.reshape(M, D)
            pdt = BF16 if latent else F32

            def rope(tables, pair, col0, col1):
                return (tables, pair, col0 // LANES, col1 // LANES, S) if latent else None

            if i % 2 == 0:
                wi = w_in_even[j]
                w_in = jnp.concatenate([wi[:, :qa] * _logit_scale(hd), wi[:, qa:]], axis=1).astype(BF16)
                proj = _matmul(hf, w_in, out_dtype=pdt, rope=rope(rope_hd, hd // 4, 0, qa + kva)).reshape(B, S, -1)
                if latent:
                    ctx = (cache_a_k[:, j].reshape(B, P, kva), cache_a_v[:, j].reshape(B, P, kva))
                else:
                    ctx = None
                    new_state["a_k"] = proj[:, :, qa:qa + kva].reshape(B, 1, S, KV, hd)
                    new_state["a_v"] = proj[:, :, qa + kva:qa + 2 * kva].reshape(B, 1, S, KV, hd)
                a = _gqa(proj, 0, proj, qa // hd, proj, (qa + kva) // hd, a_sink[j], ctx, KV=KV, G=G, hd=hd)
                bmix = _pool(proj, (qa + 2 * kva) // Cg, w_pool[j], pool_scale[j])
                y = _matmul_parts([a.reshape(M, qa), bmix.reshape(M, pool_dim)], w_out_even[j].astype(BF16))
            else:
                lam_init = 0.8 - 0.6 * math.exp(-0.3 * i)
                wi = w_in_odd[j]
                c1, c2 = CQ + CKV, CQ + CKV + CR
                w_main = jnp.concatenate([wi[:, :c1], wi[:, c2:c2 + dqk] * _logit_scale(DK), wi[:, c2 + dqk:]],
                                         axis=1).astype(BF16)
                w_kr = jnp.concatenate([wi[:, c1:c2], jnp.zeros((D, LANES - CR), F32)], axis=1).astype(BF16)
                o_dq, o_dk, o_dv = c1, c1 + dqk, c1 + 2 * dqk
                proj = _matmul(hf, w_main, out_dtype=pdt, rope=rope(rope_hd, hd // 4, o_dq, o_dv))
                krp = _matmul(hf, w_kr, out_dtype=pdt, tn_pref=LANES,
                              rope=rope(rope_cr, CR // 4, 0, LANES)).reshape(B, S, LANES)
                (cqn,) = _rms_cols(proj, 0, g_q_norm[j], [BF16])
                ckvn_f, ckvn_b = _rms_cols(proj, CQ // CKV, g_kv_norm[j], [F32, BF16])
                wq = w_uq[j].reshape(CQ, CH, CN + CR)
                wq = jnp.concatenate([wq[:, :, :CN].reshape(CQ, CH * CN), wq[:, :, CN:].reshape(CQ, CH * CR)], axis=1)
                wq = wq * _logit_scale(CN + CR)
                qall = _matmul(cqn, wq.astype(BF16), out_dtype=pdt,
                               rope=rope(rope_cr, CR // 4, CH * CN, CH * (CN + CR))).reshape(B, S, CH * (CN + CR))
                wkv = jnp.concatenate([w_uk[j].reshape(CKV, CH * CN), w_uv[j].reshape(CKV, CH * CV)], axis=1).astype(BF16)
                proj3 = proj.reshape(B, S, -1)
                if latent:
                    ckv_all = jnp.concatenate([ckvn_b.reshape(B, S, CKV), cache_c_ckv[:, j].astype(BF16)], axis=1)
                    kr_all = jnp.concatenate([krp[:, :, :CR], cache_c_krope[:, j].astype(BF16)], axis=1)
                    d_ctx = (cache_d_k[:, j].reshape(B, P, dqk), cache_d_v[:, j].reshape(B, P, DH * DV))
                else:
                    ckv_all = ckvn_b.reshape(B, S, CKV)
                    kr_all = krp[:, :, :CR].astype(BF16)
                    d_ctx = None
                    new_state["c_ckv"] = ckvn_f.reshape(B, 1, S, CKV)
                    new_state["c_krope"] = krp[:, :, :CR].reshape(B, 1, S, CR)
                    new_state["d_k"] = proj3[:, :, o_dk:o_dk + dqk].reshape(B, 1, S, DH, 2, DK)
                    new_state["d_v"] = proj3[:, :, o_dv:o_dv + dqk].reshape(B, 1, S, DH, DV)
                Sk = ckv_all.shape[1]
                knv = _matmul(ckv_all.reshape(B * Sk, CKV), wkv, out_dtype=BF16).reshape(B, Sk, -1)
                zk = jnp.zeros_like(kr_all)
                kr2 = jnp.concatenate([kr_all, zk, zk, kr_all], axis=-1)
                c_out = _mla(qall, 0, qall, CH * CN // LANES, knv, kr2, H=CH, dn=CN, dr=CR)
                d_out = _diff(proj3, o_dq // DV, proj3, o_dk // DV, proj3, o_dv // DV, d_ctx,
                              lambda_q1[j], lambda_k1[j], lambda_q2[j], lambda_k2[j], g_subln[j], lam_init, H=DH, dk=DK)
                y = _matmul_parts([c_out.reshape(M, -1), d_out.reshape(M, -1)], w_out_odd[j].astype(BF16))
            x, h = _resid_norm(x, row0, stride, resid=(y, g_post_mix[i], mods[i], G_M),
                               norm=(g_pre_ffn[i], mods[i], SC_F, SH_F))
            hmid = _ffn_up(h.reshape(M, D), S, w_up[i].astype(BF16), conv_w[i], conv_b[i])
            f = _matmul(hmid, w_down[i].astype(BF16), tm_pref=512, tn_pref=512)
            if i + 1 < depth:
                x, h = _resid_norm(x, row0, stride, resid=(f, g_post_ffn[i], mods[i], G_F),
                                   norm=(g_pre_mix[i + 1], mods[i + 1], SC_M, SH_M))
            else:
                (x,) = _resid_norm(x, row0, stride, resid=(f, g_post_ffn[i], mods[i], G_F))
        grp["out"] = x

    return (groups[0]["out"], groups[1]["out"], new_state["a_k"], new_state["a_v"], new_state["c_ckv"],
            new_state["c_krope"], new_state["d_k"], new_state["d_v"])
```

```python
import functools
import math

import jax
import jax.numpy as jnp
from jax import lax
from jax.experimental import pallas as pl
from jax.experimental.pallas import tpu as pltpu

GRID_W = 64
Q_BLOCK = 128
WINDOW = 128
ROPE_BASE = 10000.0
EPS = 1e-6
NEG_INF = -1e30
POOL_WINDOWS = (2, 4, 8, 16)

LANES = 128
BF16_ROWS = 16
VMEM_LIMIT = 56 * 1024 * 1024

BF16 = jnp.bfloat16
F32 = jnp.float32


def _tile(dim, pref, align):
    best = None
    t = align
    while t <= min(dim, pref):
        if dim % t == 0:
            best = t
        t += align
    return dim if best is None else best


def _params(*sem):
    return pltpu.CompilerParams(dimension_semantics=sem, vmem_limit_bytes=VMEM_LIMIT)


def _mods_kernel(c_ref, w_ref, b_ref, o_ref):
    c = c_ref[...]
    s = (c * (1.0 / (1.0 + jnp.exp(-c)))).astype(BF16)
    o_ref[...] = jnp.dot(s, w_ref[...].astype(BF16), preferred_element_type=F32) + b_ref[...]


def _mods(cond, w_mod, b_mod):
    depth, D, N = w_mod.shape
    R = cond.shape[0]
    tn = _tile(N, 512, LANES)
    return pl.pallas_call(
        _mods_kernel,
        grid=(depth, N // tn),
        in_specs=[
            pl.BlockSpec((R, D), lambda l, j: (0, 0)),
            pl.BlockSpec((None, D, tn), lambda l, j: (l, 0, j)),
            pl.BlockSpec((None, 1, tn), lambda l, j: (l, 0, j)),
        ],
        out_specs=pl.BlockSpec((None, R, tn), lambda l, j: (l, 0, j)),
        out_shape=jax.ShapeDtypeStruct((depth, R, N), F32),
        compiler_params=_params("parallel", "parallel"),
        name="mods",
    )(cond, w_mod, b_mod.reshape(depth, 1, N))


def _rms(x, g):
    return x * lax.rsqrt(jnp.mean(x * x, axis=-1, keepdims=True) + EPS) * g


def _resid_norm_kernel(*refs, has_resid, has_norm):
    refs = list(refs)
    x = refs.pop(0)[0]
    if has_resid:
        y_ref, gpost_ref, gate_ref = refs[:3]
        refs = refs[3:]
        x = x + gate_ref[...] * _rms(y_ref[0], gpost_ref[...])
    if has_norm:
        gpre_ref, sc_ref, sh_ref = refs[:3]
        refs = refs[3:]
    if has_resid:
        refs.pop(0)[0] = x
    if has_norm:
        h = _rms(x, gpre_ref[...]) * (1.0 + sc_ref[...]) + sh_ref[...]
        refs.pop(0)[0] = h.astype(BF16)


def _resid_norm(x, row0, row_stride, *, resid=None, norm=None):
    B, S, D = x.shape
    ts = _tile(S, 256, 8)
    xspec = pl.BlockSpec((1, ts, D), lambda b, s: (b, s, 0))
    gspec = pl.BlockSpec((1, D), lambda b, s: (0, 0))

    def mspec(col):
        return pl.BlockSpec((None, 1, D), lambda b, s: (row0 + b * row_stride, 0, col))

    args, in_specs, out_shape, out_specs = [x], [xspec], [], []
    if resid is not None:
        y, gpost, mods, gate_col = resid
        args += [y.reshape(B, S, D), gpost.reshape(1, D), mods]
        in_specs += [xspec, gspec, mspec(gate_col)]
        out_shape.append(jax.ShapeDtypeStruct((B, S, D), F32))
        out_specs.append(xspec)
    if norm is not None:
        gpre, mods, sc_col, sh_col = norm
        args += [gpre.reshape(1, D), mods, mods]
        in_specs += [gspec, mspec(sc_col), mspec(sh_col)]
        out_shape.append(jax.ShapeDtypeStruct((B, S, D), BF16))
        out_specs.append(xspec)
    outs = pl.pallas_call(
        functools.partial(_resid_norm_kernel, has_resid=resid is not None, has_norm=norm is not None),
        grid=(B, S // ts),
        in_specs=in_specs,
        out_specs=out_specs,
        out_shape=out_shape,
        compiler_params=_params("parallel", "parallel"),
        name="resid_norm",
    )(*args)
    return outs


def _mm_kernel(x_ref, w_ref, o_ref):
    o_ref[...] = jnp.dot(x_ref[...].astype(BF16), w_ref[...], preferred_element_type=F32).astype(o_ref.dtype)


def _mm_parts_kernel(*refs):
    o_ref = refs[-1]
    acc = None
    for x_ref, w_ref in zip(refs[0:-1:2], refs[1:-1:2]):
        d = jnp.dot(x_ref[...], w_ref[...], preferred_element_type=F32)
        acc = d if acc is None else acc + d
    o_ref[...] = acc.astype(o_ref.dtype)


def _matmul_parts(xs, w, *, out_dtype=F32, tm_pref=1024, tn_pref=512):
    M = xs[0].shape[0]
    N = w.shape[1]
    starts = [sum(x.shape[1] for x in xs[:n]) for n in range(len(xs))]
    if any(k0 % x.shape[1] for k0, x in zip(starts, xs)):
        return _matmul(jnp.concatenate(xs, axis=1), w, out_dtype=out_dtype, tm_pref=tm_pref, tn_pref=tn_pref)
    tm = _tile(M, tm_pref, BF16_ROWS)
    tn = _tile(N, tn_pref, LANES)
    in_specs, args, k0 = [], [], 0
    for x in xs:
        kp = x.shape[1]
        in_specs += [pl.BlockSpec((tm, kp), lambda i, j: (i, 0)),
                     pl.BlockSpec((kp, tn), lambda i, j, kb=k0 // kp: (kb, j))]
        args += [x, w]
        k0 += kp
    return pl.pallas_call(
        _mm_parts_kernel,
        grid=(M // tm, N // tn),
        in_specs=in_specs,
        out_specs=pl.BlockSpec((tm, tn), lambda i, j: (i, j)),
        out_shape=jax.ShapeDtypeStruct((M, N), out_dtype),
        compiler_params=_params("parallel", "parallel"),
        name="matmul_parts",
    )(*args)


def _rotate_slab(x, cos, sin, first, pair):
    partner = jnp.where(first, pltpu.roll(x, LANES - pair, 1), pltpu.roll(x, pair, 1))
    return x * cos + partner * sin


def _mm_rope_kernel(x_ref, w_ref, cos_ref, sin_ref, o_ref, *, pair, g0, g1):
    tm, tn = o_ref.shape
    spt = tn // LANES
    acc = jnp.dot(x_ref[...].astype(BF16), w_ref[...], preferred_element_type=F32)
    j = pl.program_id(1)
    lane = lax.broadcasted_iota(jnp.int32, (tm, LANES), 1)
    first = (lane % (2 * pair)) < pair

    def emit(rotated):
        for c in range(spt):
            x = acc[:, c * LANES:(c + 1) * LANES]
            if rotated[c]:
                x = _rotate_slab(x, cos_ref[...], sin_ref[...], first, pair)
            o_ref[:, c * LANES:(c + 1) * LANES] = x.astype(o_ref.dtype)

    full_lo, full_hi = -(-g0 // spt), g1 // spt
    mixed = [t for t in {g0 // spt, (g1 - 1) // spt} if not full_lo <= t < full_hi]
    is_full = (j >= full_lo) & (j < full_hi)
    is_mixed = functools.reduce(jnp.logical_or, [j == t for t in mixed], jnp.bool_(False))
    pl.when(is_full)(lambda: emit([True] * spt))
    for t in mixed:
        pl.when(j == t)(lambda t=t: emit([g0 <= t * spt + c < g1 for c in range(spt)]))
    pl.when(jnp.logical_not(is_full | is_mixed))(lambda: emit([False] * spt))


def _matmul(x, w, *, layer=None, out_dtype=F32, tm_pref=1024, tn_pref=512, rope=None):
    M, K = x.shape
    N = w.shape[-1]
    tn = _tile(N, tn_pref, LANES)
    if layer is None:
        wspec = pl.BlockSpec((K, tn), lambda i, j: (0, j))
    else:
        wspec = pl.BlockSpec((None, K, tn), lambda i, j: (layer, 0, j))
    if rope is None:
        tm = _tile(M, tm_pref, BF16_ROWS)
        kern, extra_specs, extra_args = _mm_kernel, [], []
    else:
        (cos, sin), pair, g0, g1, S = rope
        tm = _tile(S, tm_pref, BF16_ROWS)
        tps = S // tm
        tspec = pl.BlockSpec((tm, LANES), lambda i, j: (i % tps, 0))
        kern = functools.partial(_mm_rope_kernel, pair=pair, g0=g0, g1=g1)
        extra_specs, extra_args = [tspec, tspec], [cos, sin]
    return pl.pallas_call(
        kern,
        grid=(M // tm, N // tn),
        in_specs=[
            pl.BlockSpec((tm, K), lambda i, j: (i, 0)),
            wspec,
        ] + extra_specs,
        out_specs=pl.BlockSpec((tm, tn), lambda i, j: (i, j)),
        out_shape=jax.ShapeDtypeStruct((M, N), out_dtype),
        compiler_params=_params("parallel", "parallel"),
        name="matmul",
    )(x, w, *extra_args)


def _rms_cols_kernel(x_ref, g_ref, *o_refs):
    y = _rms(x_ref[...].astype(F32), g_ref[...])
    for o_ref in o_refs:
        o_ref[...] = y.astype(o_ref.dtype)


def _rms_cols(x, col_block, g, out_dtypes):
    M = x.shape[0]
    W = g.shape[-1]
    tm = _tile(M, 1024, BF16_ROWS)
    spec = pl.BlockSpec((tm, W), lambda i: (i, 0))
    return pl.pallas_call(
        _rms_cols_kernel,
        grid=(M // tm,),
        in_specs=[pl.BlockSpec((tm, W), lambda i: (i, col_block)), pl.BlockSpec((1, W), lambda i: (0, 0))],
        out_specs=[spec] * len(out_dtypes),
        out_shape=[jax.ShapeDtypeStruct((M, W), dt) for dt in out_dtypes],
        compiler_params=_params("parallel"),
        name="rms_cols",
    )(x, g.reshape(1, W))


def _rope_tables(S, width):
    d = width // 2
    half = d // 2
    inv = ROPE_BASE ** (-jnp.arange(half, dtype=F32) / half)
    t = jnp.arange(S)
    row = (t // GRID_W).astype(F32)[:, None] * inv[None, :]
    col = (t % GRID_W).astype(F32)[:, None] * inv[None, :]
    cos = jnp.concatenate([jnp.cos(row), jnp.cos(row), jnp.cos(col), jnp.cos(col)], axis=-1)
    sin = jnp.concatenate([-jnp.sin(row), jnp.sin(row), -jnp.sin(col), jnp.sin(col)], axis=-1)
    reps = LANES // width
    return jnp.tile(cos, (1, reps)), jnp.tile(sin, (1, reps))


LOG2E = math.log2(math.e)


def _logit_scale(d):
    return d ** -0.5 * LOG2E


def _dot_nt(a, b):
    return lax.dot_general(a, b, (((1,), (1,)), ((), ())), preferred_element_type=F32)


def _gqa_kernel(*refs, G, hd, tq, S, windowed):
    if windowed:
        q_ref, k_ref, v_ref, sink_ref, ck_ref, cv_ref, o_ref = refs
    else:
        q_ref, k_ref, v_ref, sink_ref, o_ref = refs
    rows = G * tq
    q = jnp.concatenate([q_ref[0, :, g * hd:(g + 1) * hd] for g in range(G)], axis=0).astype(BF16)
    sink = jnp.concatenate([jnp.broadcast_to(sink_ref[0, g:g + 1, :], (tq, LANES)) for g in range(G)], axis=0) * LOG2E
    if windowed:
        n = pl.program_id(2)
        span = 3 * Q_BLOCK
        start = pl.multiple_of(jnp.clip(n * tq - Q_BLOCK, 0, S - span), Q_BLOCK)
        kw = k_ref[0, pl.ds(start, span), :].astype(BF16)
        vw = v_ref[0, pl.ds(start, span), :].astype(BF16)
        qpos = n * tq + lax.broadcasted_iota(jnp.int32, (rows, span), 0) % tq
        kpos = start + lax.broadcasted_iota(jnp.int32, (rows, span), 1)
        s = jnp.where(jnp.abs(kpos - qpos) <= WINDOW, _dot_nt(q, kw), NEG_INF)
        sc = _dot_nt(q, ck_ref[0].astype(BF16))
        m = jnp.maximum(jnp.max(s, axis=-1, keepdims=True), jnp.max(sc, axis=-1, keepdims=True))
    else:
        kw = k_ref[0].astype(BF16)
        vw = v_ref[0].astype(BF16)
        s = _dot_nt(q, kw)
        m = jnp.max(s, axis=-1, keepdims=True)
    p = jnp.exp2(s - m)
    l = jnp.sum(p, axis=-1, keepdims=True)
    acc = jnp.dot(p.astype(BF16), vw, preferred_element_type=F32)
    if windowed:
        pc = jnp.exp2(sc - m)
        l = l + jnp.sum(pc, axis=-1, keepdims=True)
        acc = acc + jnp.dot(pc.astype(BF16), cv_ref[0].astype(BF16), preferred_element_type=F32)
    out = acc / (l + jnp.exp2(sink - m))
    for g in range(G):
        o_ref[0, :, g * hd:(g + 1) * hd] = out[g * tq:(g + 1) * tq].astype(o_ref.dtype)


def _gqa(q, q_cb0, k, k_cb0, v, v_cb0, sink, ctx, *, KV, G, hd):
    B, S, _ = q.shape
    windowed = ctx is not None
    tq = Q_BLOCK if windowed else _tile(S, 256, BF16_ROWS)
    sink3 = jnp.broadcast_to(sink.astype(F32).reshape(KV, G, 1), (KV, G, LANES))
    in_specs = [
        pl.BlockSpec((1, tq, G * hd), lambda b, h, n: (b, n, q_cb0 + h)),
        pl.BlockSpec((1, S, hd), lambda b, h, n: (b, 0, k_cb0 + h)),
        pl.BlockSpec((1, S, hd), lambda b, h, n: (b, 0, v_cb0 + h)),
        pl.BlockSpec((1, G, LANES), lambda b, h, n: (h, 0, 0)),
    ]
    args = [q, k, v, sink3]
    if windowed:
        ck, cv = ctx
        P = ck.shape[1]
        cspec = pl.BlockSpec((1, P, hd), lambda b, h, n: (b, 0, h))
        in_specs += [cspec, cspec]
        args += [ck, cv]
    return pl.pallas_call(
        functools.partial(_gqa_kernel, G=G, hd=hd, tq=tq, S=S, windowed=windowed),
        grid=(B, KV, S // tq),
        in_specs=in_specs,
        out_specs=pl.BlockSpec((1, tq, G * hd), lambda b, h, n: (b, n, h)),
        out_shape=jax.ShapeDtypeStruct((B, S, KV * G * hd), BF16),
        compiler_params=_params("parallel", "parallel", "parallel"),
        name="gqa_attention",
    )(*args)


POOL_PAD = 16


def _pool_kernel(u_ref, w_ref, ps_ref, o_ref, pad_ref, *, S):
    grp = pl.program_id(1)
    u = u_ref[0].astype(F32)
    C = u.shape[-1]
    pad_ref[0:POOL_PAD, :] = jnp.zeros((POOL_PAD, C), F32)
    pad_ref[POOL_PAD + S:2 * POOL_PAD + S, :] = jnp.zeros((POOL_PAD, C), F32)
    pad_ref[POOL_PAD:POOL_PAD + S, :] = u
    t = lax.broadcasted_iota(jnp.int32, (S, 1), 0)
    for gi, win in enumerate(POOL_WINDOWS):
        @pl.when(grp == gi)
        def _(win=win):
            half = win // 2
            total = pad_ref[POOL_PAD - half:POOL_PAD - half + S, :]
            for off in range(-half + 1, half):
                total = total + pad_ref[POOL_PAD + off:POOL_PAD + off + S, :]
            count = (jnp.clip(t + half, 0, S) - jnp.clip(t - half, 0, S)).astype(F32)
            pooled = (total / count - u).astype(BF16)
            y = jnp.dot(pooled, w_ref[0].astype(BF16), preferred_element_type=F32) * ps_ref[...]
            o_ref[0] = y.astype(o_ref.dtype)


def _pool(u, u_cb0, w_pool, pool_scale):
    B, S, _ = u.shape
    NG, Cg, _ = w_pool.shape
    assert NG == len(POOL_WINDOWS)
    return pl.pallas_call(
        functools.partial(_pool_kernel, S=S),
        grid=(B, NG),
        in_specs=[
            pl.BlockSpec((1, S, Cg), lambda b, g: (b, 0, u_cb0 + g)),
            pl.BlockSpec((1, Cg, Cg), lambda b, g: (g, 0, 0)),
            pl.BlockSpec((1, Cg), lambda b, g: (0, g)),
        ],
        out_specs=pl.BlockSpec((1, S, Cg), lambda b, g: (b, 0, g)),
        out_shape=jax.ShapeDtypeStruct((B, S, NG * Cg), BF16),
        scratch_shapes=[pltpu.VMEM((S + 2 * POOL_PAD, Cg), F32)],
        compiler_params=_params("parallel", "parallel"),
        name="pool_mixer",
    )(u, w_pool, pool_scale.reshape(1, NG * Cg))


def _mla_kernel(qn_ref, qr_ref, kn_ref, kr_ref, v_ref, o_ref, *, dn):
    qr = qr_ref[0].astype(BF16)
    for e in range(2):
        q = jnp.concatenate([qn_ref[0, :, e * dn:(e + 1) * dn].astype(BF16), qr], axis=-1)
        k = jnp.concatenate([kn_ref[0, :, e * dn:(e + 1) * dn], kr_ref[0, :, e * LANES:(e + 1) * LANES]], axis=-1)
        s = _dot_nt(q, k)
        m = jnp.max(s, axis=-1, keepdims=True)
        p = jnp.exp2(s - m)
        l = jnp.sum(p, axis=-1, keepdims=True)
        acc = jnp.dot(p.astype(BF16), v_ref[0, :, e * dn:(e + 1) * dn], preferred_element_type=F32)
        o_ref[0, :, e * dn:(e + 1) * dn] = (acc / l).astype(o_ref.dtype)


def _mla(qn, qn_cb0, qr, qr_cb0, knv, kr2, *, H, dn, dr):
    B, S, _ = qn.shape
    Sk = knv.shape[1]
    assert dn == LANES and 2 * dr == LANES
    tq = _tile(S, 256, BF16_ROWS)
    HP = H // 2
    return pl.pallas_call(
        functools.partial(_mla_kernel, dn=dn),
        grid=(B, HP, S // tq),
        in_specs=[
            pl.BlockSpec((1, tq, 2 * dn), lambda b, h, n: (b, n, qn_cb0 + h)),
            pl.BlockSpec((1, tq, LANES), lambda b, h, n: (b, n, qr_cb0 + h)),
            pl.BlockSpec((1, Sk, 2 * dn), lambda b, h, n: (b, 0, h)),
            pl.BlockSpec((1, Sk, 2 * LANES), lambda b, h, n: (b, 0, 0)),
            pl.BlockSpec((1, Sk, 2 * dn), lambda b, h, n: (b, 0, HP + h)),
        ],
        out_specs=pl.BlockSpec((1, tq, 2 * dn), lambda b, h, n: (b, n, h)),
        out_shape=jax.ShapeDtypeStruct((B, S, H * dn), BF16),
        compiler_params=_params("parallel", "parallel", "parallel"),
        name="mla_attention",
    )(qn, qr, knv, kr2, knv)


def _diff_kernel(*refs, dk, lam_init, has_ctx):
    if has_ctx:
        q_ref, k_ref, v_ref, ck_ref, cv_ref, l1_ref, l2_ref, gs_ref, o_ref = refs
        key_refs, val_refs = (k_ref, ck_ref), (v_ref, cv_ref)
    else:
        q_ref, k_ref, v_ref, l1_ref, l2_ref, gs_ref, o_ref = refs
        key_refs, val_refs = (k_ref,), (v_ref,)
    lam = (jnp.exp(jnp.sum(l1_ref[0:1, :] * l1_ref[1:2, :], axis=-1, keepdims=True))
           - jnp.exp(jnp.sum(l2_ref[0:1, :] * l2_ref[1:2, :], axis=-1, keepdims=True)) + lam_init)
    es, fs = [], []
    for mi in range(2):
        q = q_ref[0, :, mi * dk:(mi + 1) * dk].astype(BF16)
        ss = [_dot_nt(q, kr[0, :, mi * dk:(mi + 1) * dk].astype(BF16)) for kr in key_refs]
        m = functools.reduce(jnp.maximum, [jnp.max(s, axis=-1, keepdims=True) for s in ss])
        e = [jnp.exp2(s - m) for s in ss]
        tot = functools.reduce(jnp.add, [jnp.sum(x, axis=-1, keepdims=True) for x in e])
        es.append(e)
        fs.append((1.0 if mi == 0 else lam) / tot)
    out = None
    for si, vr in enumerate(val_refs):
        w = (es[0][si] * fs[0] - es[1][si] * fs[1]).astype(BF16)
        o = jnp.dot(w, vr[0].astype(BF16), preferred_element_type=F32)
        out = o if out is None else out + o
    o_ref[0] = (_rms(out, gs_ref[...]) * (1.0 - lam_init)).astype(o_ref.dtype)


def _diff(q, q_cb0, k, k_cb0, v, v_cb0, ctx, lam_q1, lam_k1, lam_q2, lam_k2, g_subln, lam_init, *, H, dk):
    B, S, _ = q.shape
    dv = 2 * dk
    tq = _tile(S, 256, BF16_ROWS)
    l1 = jnp.stack([lam_q1, lam_k1]).astype(F32)
    l2 = jnp.stack([lam_q2, lam_k2]).astype(F32)
    lspec = pl.BlockSpec((2, dk), lambda b, h, n: (0, 0))
    in_specs = [
        pl.BlockSpec((1, tq, dv), lambda b, h, n: (b, n, q_cb0 + h)),
        pl.BlockSpec((1, S, dv), lambda b, h, n: (b, 0, k_cb0 + h)),
        pl.BlockSpec((1, S, dv), lambda b, h, n: (b, 0, v_cb0 + h)),
    ]
    args = [q, k, v]
    if ctx is not None:
        P = ctx[0].shape[1]
        cspec = pl.BlockSpec((1, P, dv), lambda b, h, n: (b, 0, h))
        in_specs += [cspec, cspec]
        args += list(ctx)
    in_specs += [lspec, lspec, pl.BlockSpec((1, dv), lambda b, h, n: (0, 0))]
    args += [l1, l2, g_subln.reshape(1, dv)]
    return pl.pallas_call(
        functools.partial(_diff_kernel, dk=dk, lam_init=lam_init, has_ctx=ctx is not None),
        grid=(B, H, S // tq),
        in_specs=in_specs,
        out_specs=pl.BlockSpec((1, tq, dv), lambda b, h, n: (b, n, h)),
        out_shape=jax.ShapeDtypeStruct((B, S, H * dv), BF16),
        compiler_params=_params("parallel", "parallel", "parallel"),
        name="diff_attention",
    )(*args)


Z_PAD = 8
FFN_ROW_CHUNK = 32
FFN_DOT_ROWS = 1024


def _ffn_up_kernel(h_ref, wg_ref, wv_ref, cwg_ref, cwv_ref, cbg_ref, cbv_ref, o_ref, z_ref, *, tm, tn, S, rc, rm):
    zero = jnp.zeros((Z_PAD, 2 * tn), F32)
    z_ref[0:Z_PAD, :] = zero
    z_ref[Z_PAD + tm:2 * Z_PAD + tm, :] = zero
    for r0 in range(0, tm, rm):
        h = h_ref[r0:r0 + rm, :]
        z_ref[Z_PAD + r0:Z_PAD + r0 + rm, 0:tn] = jnp.dot(h, wg_ref[...], preferred_element_type=F32)
        z_ref[Z_PAD + r0:Z_PAD + r0 + rm, tn:2 * tn] = jnp.dot(h, wv_ref[...], preferred_element_type=F32)

    cw = jnp.concatenate([cwg_ref[...], cwv_ref[...]], axis=1)
    cb = jnp.concatenate([cbg_ref[...], cbv_ref[...]], axis=1)
    span = rc + 2 * Z_PAD
    for r0 in range(0, tm, rc):
        zf = z_ref[r0:r0 + span, :]
        zc = zf[Z_PAD:Z_PAD + rc]
        zp = pltpu.roll(zf, 1, 0)[Z_PAD:Z_PAD + rc]
        zn = pltpu.roll(zf, span - 1, 0)[Z_PAD:Z_PAD + rc]
        if tm > S:
            pos = (r0 + lax.broadcasted_iota(jnp.int32, (rc, 1), 0)) % S
            zp = jnp.where(pos != 0, zp, 0.0)
            zn = jnp.where(pos != S - 1, zn, 0.0)
        y = zp * cw[0:1, :] + zc * cw[1:2, :] + zn * cw[2:3, :] + cb
        g = y[:, 0:tn]
        o_ref[r0:r0 + rc, :] = (g * (1.0 / (1.0 + jnp.exp(-g))) * y[:, tn:2 * tn]).astype(o_ref.dtype)


def _ffn_up(h, S, w_up, conv_w, conv_b, layer):
    M, D = h.shape
    L = w_up.shape[0]
    F = w_up.shape[2] // 2
    tm = _tile(M, max(2048, S), S)
    assert tm % S == 0
    tn = _tile(F, 256, LANES)
    nj = F // tn
    rc = _tile(tm, FFN_ROW_CHUNK, 8)
    rm = _tile(tm, FFN_DOT_ROWS, BF16_ROWS)
    cb = conv_b.reshape(L, 1, 2 * F)
    gate_tile = lambda i, j: (layer, 0, j)
    val_tile = lambda i, j: (layer, 0, nj + j)
    return pl.pallas_call(
        functools.partial(_ffn_up_kernel, tm=tm, tn=tn, S=S, rc=rc, rm=rm),
        grid=(M // tm, nj),
        in_specs=[
            pl.BlockSpec((tm, D), lambda i, j: (i, 0)),
            pl.BlockSpec((None, D, tn), gate_tile),
            pl.BlockSpec((None, D, tn), val_tile),
            pl.BlockSpec((None, 3, tn), gate_tile),
            pl.BlockSpec((None, 3, tn), val_tile),
            pl.BlockSpec((None, 1, tn), gate_tile),
            pl.BlockSpec((None, 1, tn), val_tile),
        ],
        out_specs=pl.BlockSpec((tm, tn), lambda i, j: (i, j)),
        out_shape=jax.ShapeDtypeStruct((M, F), BF16),
        scratch_shapes=[pltpu.VMEM((tm + 2 * Z_PAD, 2 * tn), F32)],
        compiler_params=_params("parallel", "parallel"),
        name="ffn_up_conv_gate",
    )(h, w_up, w_up, conv_w, conv_w, cb, cb)


def kernel(x_prompt, x_sample, cache_a_k, cache_a_v, cache_c_ckv, cache_c_krope, cache_d_k, cache_d_v, c, c_ctx, w_mod, b_mod, g_pre_mix, g_post_mix, g_pre_ffn, g_post_ffn, w_in_even, w_out_even, a_sink, w_pool, pool_scale, w_in_odd, w_out_odd, g_q_norm, w_uq, g_kv_norm, w_uk, w_uv, lambda_q1, lambda_k1, lambda_q2, lambda_k2, g_subln, w_up, conv_w, conv_b, w_down):
    B1, S1, D = x_prompt.shape
    B2, S2, _ = x_sample.shape
    depth = w_mod.shape[0]
    P = cache_a_k.shape[2]
    KV, hd = cache_a_k.shape[3], cache_a_k.shape[4]
    AH = a_sink.shape[1]
    G = AH // KV
    NG, Cg = w_pool.shape[1], w_pool.shape[2]
    qa, kva, pool_dim = AH * hd, KV * hd, NG * Cg
    CQ = g_q_norm.shape[1]
    CKV, CH, CN = w_uk.shape[1], w_uk.shape[2], w_uk.shape[3]
    CR = cache_c_krope.shape[3]
    CV = w_uv.shape[3]
    DH, DK = cache_d_k.shape[3], cache_d_k.shape[5]
    DV = 2 * DK
    dqk = DH * 2 * DK
    assert hd == LANES and CN == LANES and CV == CN and 2 * CR == LANES and DK == LANES

    R = -(-(B2 + 1) // 8) * 8
    cond = jnp.zeros((R, D), F32).at[:B2].set(c).at[B2].set(c_ctx)
    mods = _mods(cond, w_mod, b_mod).reshape(depth, R, 1, 6 * D)
    SH_M, SC_M, G_M, SH_F, SC_F, G_F = range(6)

    groups = [
        dict(x=x_prompt, B=B1, S=S1, row0=B2, stride=0, latent=False),
        dict(x=x_sample, B=B2, S=S2, row0=0, stride=1, latent=True),
    ]
    rope_hd = _rope_tables(S2, hd)
    rope_cr = _rope_tables(S2, CR)
    w_up_b = w_up.astype(BF16)
    w_down_b = w_down.astype(BF16)

    new_state = {}
    for grp in groups:
        B, S, latent = grp["B"], grp["S"], grp["latent"]
        M = B * S
        row0, stride = grp["row0"], grp["stride"]
        x = grp["x"]
        (h,) = _resid_norm(x, row0, stride, norm=(g_pre_mix[0], mods[0], SC_M, SH_M))
        for i in range(depth):
            j = i // 2
            hf = h.reshape(M, D)
            pdt = BF16 if latent else F32

            def rope(tables, pair, col0, col1):
                return (tables, pair, col0 // LANES, col1 // LANES, S) if latent else None

            if i % 2 == 0:
                wi = w_in_even[j]
                w_in = jnp.concatenate([wi[:, :qa] * _logit_scale(hd), wi[:, qa:]], axis=1).astype(BF16)
                proj = _matmul(hf, w_in, out_dtype=pdt, rope=rope(rope_hd, hd // 4, 0, qa + kva)).reshape(B, S, -1)
                if latent:
                    ctx = (cache_a_k[:, j].reshape(B, P, kva), cache_a_v[:, j].reshape(B, P, kva))
                else:
                    ctx = None
                    new_state["a_k"] = proj[:, :, qa:qa + kva].reshape(B, 1, S, KV, hd)
                    new_state["a_v"] = proj[:, :, qa + kva:qa + 2 * kva].reshape(B, 1, S, KV, hd)
                a = _gqa(proj, 0, proj, qa // hd, proj, (qa + kva) // hd, a_sink[j], ctx, KV=KV, G=G, hd=hd)
                bmix = _pool(proj, (qa + 2 * kva) // Cg, w_pool[j], pool_scale[j])
                y = _matmul_parts([a.reshape(M, qa), bmix.reshape(M, pool_dim)], w_out_even[j].astype(BF16))
            else:
                lam_init = 0.8 - 0.6 * math.exp(-0.3 * i)
                wi = w_in_odd[j]
                c1, c2 = CQ + CKV, CQ + CKV + CR
                w_main = jnp.concatenate([wi[:, :c1], wi[:, c2:c2 + dqk] * _logit_scale(DK), wi[:, c2 + dqk:]],
                                         axis=1).astype(BF16)
                w_kr = jnp.concatenate([wi[:, c1:c2], jnp.zeros((D, LANES - CR), F32)], axis=1).astype(BF16)
                o_dq, o_dk, o_dv = c1, c1 + dqk, c1 + 2 * dqk
                proj = _matmul(hf, w_main, out_dtype=pdt, rope=rope(rope_hd, hd // 4, o_dq, o_dv))
                krp = _matmul(hf, w_kr, out_dtype=pdt, tn_pref=LANES,
                              rope=rope(rope_cr, CR // 4, 0, LANES)).reshape(B, S, LANES)
                (cqn,) = _rms_cols(proj, 0, g_q_norm[j], [BF16])
                ckvn_f, ckvn_b = _rms_cols(proj, CQ // CKV, g_kv_norm[j], [F32, BF16])
                wq = w_uq[j].reshape(CQ, CH, CN + CR)
                wq = jnp.concatenate([wq[:, :, :CN].reshape(CQ, CH * CN), wq[:, :, CN:].reshape(CQ, CH * CR)], axis=1)
                wq = wq * _logit_scale(CN + CR)
                qall = _matmul(cqn, wq.astype(BF16), out_dtype=pdt,
                               rope=rope(rope_cr, CR // 4, CH * CN, CH * (CN + CR))).reshape(B, S, CH * (CN + CR))
                wkv = jnp.concatenate([w_uk[j].reshape(CKV, CH * CN), w_uv[j].reshape(CKV, CH * CV)], axis=1).astype(BF16)
                proj3 = proj.reshape(B, S, -1)
                if latent:
                    ckv_all = jnp.concatenate([ckvn_b.reshape(B, S, CKV), cache_c_ckv[:, j].astype(BF16)], axis=1)
                    kr_all = jnp.concatenate([krp[:, :, :CR], cache_c_krope[:, j].astype(BF16)], axis=1)
                    d_ctx = (cache_d_k[:, j].reshape(B, P, dqk), cache_d_v[:, j].reshape(B, P, DH * DV))
                else:
                    ckv_all = ckvn_b.reshape(B, S, CKV)
                    kr_all = krp[:, :, :CR].astype(BF16)
                    d_ctx = None
                    new_state["c_ckv"] = ckvn_f.reshape(B, 1, S, CKV)
                    new_state["c_krope"] = krp[:, :, :CR].reshape(B, 1, S, CR)
                    new_state["d_k"] = proj3[:, :, o_dk:o_dk + dqk].reshape(B, 1, S, DH, 2, DK)
                    new_state["d_v"] = proj3[:, :, o_dv:o_dv + dqk].reshape(B, 1, S, DH, DV)
                Sk = ckv_all.shape[1]
                knv = _matmul(ckv_all.reshape(B * Sk, CKV), wkv, out_dtype=BF16).reshape(B, Sk, -1)
                zk = jnp.zeros_like(kr_all)
                kr2 = jnp.concatenate([kr_all, zk, zk, kr_all], axis=-1)
                c_out = _mla(qall, 0, qall, CH * CN // LANES, knv, kr2, H=CH, dn=CN, dr=CR)
                d_out = _diff(proj3, o_dq // DV, proj3, o_dk // DV, proj3, o_dv // DV, d_ctx,
                              lambda_q1[j], lambda_k1[j], lambda_q2[j], lambda_k2[j], g_subln[j], lam_init, H=DH, dk=DK)
                y = _matmul_parts([c_out.reshape(M, -1), d_out.reshape(M, -1)], w_out_odd[j].astype(BF16))
            x, h = _resid_norm(x, row0, stride, resid=(y, g_post_mix[i], mods[i], G_M),
                               norm=(g_pre_ffn[i], mods[i], SC_F, SH_F))
            hmid = _ffn_up(h.reshape(M, D), S, w_up_b, conv_w, conv_b, i)
            f = _matmul(hmid, w_down_b, layer=i, tm_pref=512, tn_pref=512)
            if i + 1 < depth:
                x, h = _resid_norm(x, row0, stride, resid=(f, g_post_ffn[i], mods[i], G_F),
                                   norm=(g_pre_mix[i + 1], mods[i + 1], SC_M, SH_M))
            else:
                (x,) = _resid_norm(x, row0, stride, resid=(f, g_post_ffn[i], mods[i], G_F))
        grp["out"] = x

    return (groups[0]["out"], groups[1]["out"], new_state["a_k"], new_state["a_v"], new_state["c_ckv"],
            new_state["c_krope"], new_state["d_k"], new_state["d_v"])
```

```python
import functools
import math

import jax
import jax.numpy as jnp
from jax import lax
from jax.experimental import pallas as pl
from jax.experimental.pallas import tpu as pltpu

GRID_W = 64
Q_BLOCK = 128
WINDOW = 128
ROPE_BASE = 10000.0
EPS = 1e-6
NEG_INF = -1e30
POOL_WINDOWS = (2, 4, 8, 16)

LANES = 128
BF16_ROWS = 16
VMEM_LIMIT = 56 * 1024 * 1024

BF16 = jnp.bfloat16
F32 = jnp.float32


def _tile(dim, pref, align):
    best = None
    t = align
    while t <= min(dim, pref):
        if dim % t == 0:
            best = t
        t += align
    return dim if best is None else best


def _params(*sem):
    return pltpu.CompilerParams(dimension_semantics=sem, vmem_limit_bytes=VMEM_LIMIT)


def _mods_kernel(c_ref, w_ref, b_ref, o_ref):
    c = c_ref[...]
    s = (c * (1.0 / (1.0 + jnp.exp(-c)))).astype(BF16)
    o_ref[...] = jnp.dot(s, w_ref[...].astype(BF16), preferred_element_type=F32) + b_ref[...]


def _mods(cond, w_mod, b_mod):
    depth, D, N = w_mod.shape
    R = cond.shape[0]
    tn = _tile(N, 512, LANES)
    return pl.pallas_call(
        _mods_kernel,
        grid=(depth, N // tn),
        in_specs=[
            pl.BlockSpec((R, D), lambda l, j: (0, 0)),
            pl.BlockSpec((None, D, tn), lambda l, j: (l, 0, j)),
            pl.BlockSpec((None, 1, tn), lambda l, j: (l, 0, j)),
        ],
        out_specs=pl.BlockSpec((None, R, tn), lambda l, j: (l, 0, j)),
        out_shape=jax.ShapeDtypeStruct((depth, R, N), F32),
        compiler_params=_params("parallel", "parallel"),
        name="mods",
    )(cond, w_mod, b_mod.reshape(depth, 1, N))


def _rms(x, g):
    return x * lax.rsqrt(jnp.mean(x * x, axis=-1, keepdims=True) + EPS) * g


def _resid_norm_kernel(*refs, has_resid, has_norm):
    refs = list(refs)
    x = refs.pop(0)[0]
    if has_resid:
        y_ref, gpost_ref, gate_ref = refs[:3]
        refs = refs[3:]
        x = x + gate_ref[...] * _rms(y_ref[0], gpost_ref[...])
    if has_norm:
        gpre_ref, sc_ref, sh_ref = refs[:3]
        refs = refs[3:]
    if has_resid:
        refs.pop(0)[0] = x
    if has_norm:
        h = _rms(x, gpre_ref[...]) * (1.0 + sc_ref[...]) + sh_ref[...]
        refs.pop(0)[0] = h.astype(BF16)


def _resid_norm(x, row0, row_stride, *, resid=None, norm=None):
    B, S, D = x.shape
    ts = _tile(S, 256, 8)
    xspec = pl.BlockSpec((1, ts, D), lambda b, s: (b, s, 0))
    gspec = pl.BlockSpec((1, D), lambda b, s: (0, 0))

    def mspec(col):
        return pl.BlockSpec((None, 1, D), lambda b, s: (row0 + b * row_stride, 0, col))

    args, in_specs, out_shape, out_specs = [x], [xspec], [], []
    if resid is not None:
        y, gpost, mods, gate_col = resid
        args += [y.reshape(B, S, D), gpost.reshape(1, D), mods]
        in_specs += [xspec, gspec, mspec(gate_col)]
        out_shape.append(jax.ShapeDtypeStruct((B, S, D), F32))
        out_specs.append(xspec)
    if norm is not None:
        gpre, mods, sc_col, sh_col = norm
        args += [gpre.reshape(1, D), mods, mods]
        in_specs += [gspec, mspec(sc_col), mspec(sh_col)]
        out_shape.append(jax.ShapeDtypeStruct((B, S, D), BF16))
        out_specs.append(xspec)
    outs = pl.pallas_call(
        functools.partial(_resid_norm_kernel, has_resid=resid is not None, has_norm=norm is not None),
        grid=(B, S // ts),
        in_specs=in_specs,
        out_specs=out_specs,
        out_shape=out_shape,
        compiler_params=_params("parallel", "parallel"),
        name="resid_norm",
    )(*args)
    return outs


def _mm_kernel(x_ref, w_ref, o_ref):
    o_ref[...] = jnp.dot(x_ref[...].astype(BF16), w_ref[...], preferred_element_type=F32).astype(o_ref.dtype)


def _mm_parts_kernel(*refs):
    o_ref = refs[-1]
    acc = None
    for x_ref, w_ref in zip(refs[0:-1:2], refs[1:-1:2]):
        d = jnp.dot(x_ref[...], w_ref[...], preferred_element_type=F32)
        acc = d if acc is None else acc + d
    o_ref[...] = acc.astype(o_ref.dtype)


def _matmul_parts(xs, w, *, out_dtype=F32, tm_pref=1024, tn_pref=512):
    M = xs[0].shape[0]
    N = w.shape[1]
    starts = [sum(x.shape[1] for x in xs[:n]) for n in range(len(xs))]
    if any(k0 % x.shape[1] for k0, x in zip(starts, xs)):
        return _matmul(jnp.concatenate(xs, axis=1), w, out_dtype=out_dtype, tm_pref=tm_pref, tn_pref=tn_pref)
    tm = _tile(M, tm_pref, BF16_ROWS)
    tn = _tile(N, tn_pref, LANES)
    in_specs, args, k0 = [], [], 0
    for x in xs:
        kp = x.shape[1]
        in_specs += [pl.BlockSpec((tm, kp), lambda i, j: (i, 0)),
                     pl.BlockSpec((kp, tn), lambda i, j, kb=k0 // kp: (kb, j))]
        args += [x, w]
        k0 += kp
    return pl.pallas_call(
        _mm_parts_kernel,
        grid=(M // tm, N // tn),
        in_specs=in_specs,
        out_specs=pl.BlockSpec((tm, tn), lambda i, j: (i, j)),
        out_shape=jax.ShapeDtypeStruct((M, N), out_dtype),
        compiler_params=_params("parallel", "parallel"),
        name="matmul_parts",
    )(*args)


def _rotate_slab(x, cos, sin, first, pair):
    partner = jnp.where(first, pltpu.roll(x, LANES - pair, 1), pltpu.roll(x, pair, 1))
    return x * cos + partner * sin


def _mm_rope_kernel(x_ref, w_ref, cos_ref, sin_ref, o_ref, *, pair, g0, g1):
    tm, tn = o_ref.shape
    spt = tn // LANES
    acc = jnp.dot(x_ref[...].astype(BF16), w_ref[...], preferred_element_type=F32)
    j = pl.program_id(1)
    lane = lax.broadcasted_iota(jnp.int32, (tm, LANES), 1)
    first = (lane % (2 * pair)) < pair

    def emit(rotated):
        for c in range(spt):
            x = acc[:, c * LANES:(c + 1) * LANES]
            if rotated[c]:
                x = _rotate_slab(x, cos_ref[...], sin_ref[...], first, pair)
            o_ref[:, c * LANES:(c + 1) * LANES] = x.astype(o_ref.dtype)

    full_lo, full_hi = -(-g0 // spt), g1 // spt
    mixed = [t for t in {g0 // spt, (g1 - 1) // spt} if not full_lo <= t < full_hi]
    is_full = (j >= full_lo) & (j < full_hi)
    is_mixed = functools.reduce(jnp.logical_or, [j == t for t in mixed], jnp.bool_(False))
    pl.when(is_full)(lambda: emit([True] * spt))
    for t in mixed:
        pl.when(j == t)(lambda t=t: emit([g0 <= t * spt + c < g1 for c in range(spt)]))
    pl.when(jnp.logical_not(is_full | is_mixed))(lambda: emit([False] * spt))


def _matmul(x, w, *, layer=None, out_dtype=F32, tm_pref=1024, tn_pref=512, rope=None):
    M, K = x.shape
    N = w.shape[-1]
    tn = _tile(N, tn_pref, LANES)
    if layer is None:
        wspec = pl.BlockSpec((K, tn), lambda i, j: (0, j))
    else:
        wspec = pl.BlockSpec((None, K, tn), lambda i, j: (layer, 0, j))
    if rope is None:
        tm = _tile(M, tm_pref, BF16_ROWS)
        kern, extra_specs, extra_args = _mm_kernel, [], []
    else:
        (cos, sin), pair, g0, g1, S = rope
        tm = _tile(S, tm_pref, BF16_ROWS)
        tps = S // tm
        tspec = pl.BlockSpec((tm, LANES), lambda i, j: (i % tps, 0))
        kern = functools.partial(_mm_rope_kernel, pair=pair, g0=g0, g1=g1)
        extra_specs, extra_args = [tspec, tspec], [cos, sin]
    return pl.pallas_call(
        kern,
        grid=(M // tm, N // tn),
        in_specs=[
            pl.BlockSpec((tm, K), lambda i, j: (i, 0)),
            wspec,
        ] + extra_specs,
        out_specs=pl.BlockSpec((tm, tn), lambda i, j: (i, j)),
        out_shape=jax.ShapeDtypeStruct((M, N), out_dtype),
        compiler_params=_params("parallel", "parallel"),
        name="matmul",
    )(x, w, *extra_args)


def _rms_cols_kernel(x_ref, g_ref, *o_refs):
    y = _rms(x_ref[...].astype(F32), g_ref[...])
    for o_ref in o_refs:
        o_ref[...] = y.astype(o_ref.dtype)


def _rms_cols(x, col_block, g, out_dtypes):
    M = x.shape[0]
    W = g.shape[-1]
    tm = _tile(M, 1024, BF16_ROWS)
    spec = pl.BlockSpec((tm, W), lambda i: (i, 0))
    return pl.pallas_call(
        _rms_cols_kernel,
        grid=(M // tm,),
        in_specs=[pl.BlockSpec((tm, W), lambda i: (i, col_block)), pl.BlockSpec((1, W), lambda i: (0, 0))],
        out_specs=[spec] * len(out_dtypes),
        out_shape=[jax.ShapeDtypeStruct((M, W), dt) for dt in out_dtypes],
        compiler_params=_params("parallel"),
        name="rms_cols",
    )(x, g.reshape(1, W))


def _rope_tables(S, width):
    d = width // 2
    half = d // 2
    inv = ROPE_BASE ** (-jnp.arange(half, dtype=F32) / half)
    t = jnp.arange(S)
    row = (t // GRID_W).astype(F32)[:, None] * inv[None, :]
    col = (t % GRID_W).astype(F32)[:, None] * inv[None, :]
    cos = jnp.concatenate([jnp.cos(row), jnp.cos(row), jnp.cos(col), jnp.cos(col)], axis=-1)
    sin = jnp.concatenate([-jnp.sin(row), jnp.sin(row), -jnp.sin(col), jnp.sin(col)], axis=-1)
    reps = LANES // width
    return jnp.tile(cos, (1, reps)), jnp.tile(sin, (1, reps))


LOG2E = math.log2(math.e)
ATTN_ROWS = 256


def _logit_scale(d):
    return d ** -0.5 * LOG2E


def _dot_nt(a, b):
    return lax.dot_general(a, b, (((1,), (1,)), ((), ())), preferred_element_type=F32)


def _gqa_kernel(*refs, G, hd, tq, nb, S, windowed):
    if windowed:
        q_ref, k_ref, v_ref, sink_ref, ck_ref, cv_ref, o_ref = refs
    else:
        q_ref, k_ref, v_ref, sink_ref, o_ref = refs
    rows = G * tq
    sink = jnp.concatenate([jnp.broadcast_to(sink_ref[0, g:g + 1, :], (tq, LANES)) for g in range(G)], axis=0) * LOG2E
    for blk in range(nb):
        r0 = blk * tq
        q = jnp.concatenate([q_ref[0, r0:r0 + tq, g * hd:(g + 1) * hd] for g in range(G)], axis=0).astype(BF16)
        if windowed:
            n = pl.program_id(2) * nb + blk
            span = 3 * Q_BLOCK
            start = pl.multiple_of(jnp.clip(n * tq - Q_BLOCK, 0, S - span), Q_BLOCK)
            kw = k_ref[0, pl.ds(start, span), :].astype(BF16)
            vw = v_ref[0, pl.ds(start, span), :].astype(BF16)
            qpos = n * tq + lax.broadcasted_iota(jnp.int32, (rows, span), 0) % tq
            kpos = start + lax.broadcasted_iota(jnp.int32, (rows, span), 1)
            s = jnp.where(jnp.abs(kpos - qpos) <= WINDOW, _dot_nt(q, kw), NEG_INF)
            sc = _dot_nt(q, ck_ref[0].astype(BF16))
            m = jnp.maximum(jnp.max(s, axis=-1, keepdims=True), jnp.max(sc, axis=-1, keepdims=True))
        else:
            kw = k_ref[0].astype(BF16)
            vw = v_ref[0].astype(BF16)
            s = _dot_nt(q, kw)
            m = jnp.max(s, axis=-1, keepdims=True)
        p = jnp.exp2(s - m)
        l = jnp.sum(p, axis=-1, keepdims=True)
        acc = jnp.dot(p.astype(BF16), vw, preferred_element_type=F32)
        if windowed:
            pc = jnp.exp2(sc - m)
            l = l + jnp.sum(pc, axis=-1, keepdims=True)
            acc = acc + jnp.dot(pc.astype(BF16), cv_ref[0].astype(BF16), preferred_element_type=F32)
        out = acc / (l + jnp.exp2(sink - m))
        for g in range(G):
            o_ref[0, r0:r0 + tq, g * hd:(g + 1) * hd] = out[g * tq:(g + 1) * tq].astype(o_ref.dtype)


def _gqa(q, q_cb0, k, k_cb0, v, v_cb0, sink, ctx, *, KV, G, hd):
    B, S, _ = q.shape
    windowed = ctx is not None
    tq = Q_BLOCK if windowed else _tile(S, 256, BF16_ROWS)
    nb = 2 if windowed and S % (2 * tq) == 0 else 1
    ts = nb * tq
    sink3 = jnp.broadcast_to(sink.astype(F32).reshape(KV, G, 1), (KV, G, LANES))
    in_specs = [
        pl.BlockSpec((1, ts, G * hd), lambda b, h, n: (b, n, q_cb0 + h)),
        pl.BlockSpec((1, S, hd), lambda b, h, n: (b, 0, k_cb0 + h)),
        pl.BlockSpec((1, S, hd), lambda b, h, n: (b, 0, v_cb0 + h)),
        pl.BlockSpec((1, G, LANES), lambda b, h, n: (h, 0, 0)),
    ]
    args = [q, k, v, sink3]
    if windowed:
        ck, cv = ctx
        P = ck.shape[1]
        cspec = pl.BlockSpec((1, P, hd), lambda b, h, n: (b, 0, h))
        in_specs += [cspec, cspec]
        args += [ck, cv]
    return pl.pallas_call(
        functools.partial(_gqa_kernel, G=G, hd=hd, tq=tq, nb=nb, S=S, windowed=windowed),
        grid=(B, KV, S // ts),
        in_specs=in_specs,
        out_specs=pl.BlockSpec((1, ts, G * hd), lambda b, h, n: (b, n, h)),
        out_shape=jax.ShapeDtypeStruct((B, S, KV * G * hd), BF16),
        compiler_params=_params("parallel", "parallel", "parallel"),
        name="gqa_attention",
    )(*args)


POOL_PAD = 16


def _pool_kernel(u_ref, w_ref, ps_ref, o_ref, pad_ref, *, S):
    grp = pl.program_id(1)
    u = u_ref[0].astype(F32)
    C = u.shape[-1]
    pad_ref[0:POOL_PAD, :] = jnp.zeros((POOL_PAD, C), F32)
    pad_ref[POOL_PAD + S:2 * POOL_PAD + S, :] = jnp.zeros((POOL_PAD, C), F32)
    pad_ref[POOL_PAD:POOL_PAD + S, :] = u
    t = lax.broadcasted_iota(jnp.int32, (S, 1), 0)
    for gi, win in enumerate(POOL_WINDOWS):
        @pl.when(grp == gi)
        def _(win=win):
            half = win // 2
            total = pad_ref[POOL_PAD - half:POOL_PAD - half + S, :]
            for off in range(-half + 1, half):
                total = total + pad_ref[POOL_PAD + off:POOL_PAD + off + S, :]
            count = (jnp.clip(t + half, 0, S) - jnp.clip(t - half, 0, S)).astype(F32)
            pooled = (total / count - u).astype(BF16)
            y = jnp.dot(pooled, w_ref[0].astype(BF16), preferred_element_type=F32) * ps_ref[...]
            o_ref[0] = y.astype(o_ref.dtype)


def _pool(u, u_cb0, w_pool, pool_scale):
    B, S, _ = u.shape
    NG, Cg, _ = w_pool.shape
    assert NG == len(POOL_WINDOWS)
    return pl.pallas_call(
        functools.partial(_pool_kernel, S=S),
        grid=(B, NG),
        in_specs=[
            pl.BlockSpec((1, S, Cg), lambda b, g: (b, 0, u_cb0 + g)),
            pl.BlockSpec((1, Cg, Cg), lambda b, g: (g, 0, 0)),
            pl.BlockSpec((1, Cg), lambda b, g: (0, g)),
        ],
        out_specs=pl.BlockSpec((1, S, Cg), lambda b, g: (b, 0, g)),
        out_shape=jax.ShapeDtypeStruct((B, S, NG * Cg), BF16),
        scratch_shapes=[pltpu.VMEM((S + 2 * POOL_PAD, Cg), F32)],
        compiler_params=_params("parallel", "parallel"),
        name="pool_mixer",
    )(u, w_pool, pool_scale.reshape(1, NG * Cg))


def _mla_kernel(qn_ref, qr_ref, kn_ref, kr_ref, v_ref, o_ref, *, dn):
    tq = qn_ref.shape[1]
    for r0 in range(0, tq, ATTN_ROWS):
        rows = slice(r0, min(r0 + ATTN_ROWS, tq))
        qr = qr_ref[0, rows, :].astype(BF16)
        for e in range(2):
            q = jnp.concatenate([qn_ref[0, rows, e * dn:(e + 1) * dn].astype(BF16), qr], axis=-1)
            k = jnp.concatenate([kn_ref[0, :, e * dn:(e + 1) * dn], kr_ref[0, :, e * LANES:(e + 1) * LANES]], axis=-1)
            s = _dot_nt(q, k)
            m = jnp.max(s, axis=-1, keepdims=True)
            p = jnp.exp2(s - m)
            l = jnp.sum(p, axis=-1, keepdims=True)
            acc = jnp.dot(p.astype(BF16), v_ref[0, :, e * dn:(e + 1) * dn], preferred_element_type=F32)
            o_ref[0, rows, e * dn:(e + 1) * dn] = (acc / l).astype(o_ref.dtype)


def _mla(qn, qn_cb0, qr, qr_cb0, knv, kr2, *, H, dn, dr):
    B, S, _ = qn.shape
    Sk = knv.shape[1]
    assert dn == LANES and 2 * dr == LANES
    tq = _tile(S, 2 * ATTN_ROWS, BF16_ROWS)
    HP = H // 2
    return pl.pallas_call(
        functools.partial(_mla_kernel, dn=dn),
        grid=(B, HP, S // tq),
        in_specs=[
            pl.BlockSpec((1, tq, 2 * dn), lambda b, h, n: (b, n, qn_cb0 + h)),
            pl.BlockSpec((1, tq, LANES), lambda b, h, n: (b, n, qr_cb0 + h)),
            pl.BlockSpec((1, Sk, 2 * dn), lambda b, h, n: (b, 0, h)),
            pl.BlockSpec((1, Sk, 2 * LANES), lambda b, h, n: (b, 0, 0)),
            pl.BlockSpec((1, Sk, 2 * dn), lambda b, h, n: (b, 0, HP + h)),
        ],
        out_specs=pl.BlockSpec((1, tq, 2 * dn), lambda b, h, n: (b, n, h)),
        out_shape=jax.ShapeDtypeStruct((B, S, H * dn), BF16),
        compiler_params=_params("parallel", "parallel", "parallel"),
        name="mla_attention",
    )(qn, qr, knv, kr2, knv)


def _diff_kernel(*refs, dk, lam_init, has_ctx):
    if has_ctx:
        q_ref, k_ref, v_ref, ck_ref, cv_ref, l1_ref, l2_ref, gs_ref, o_ref = refs
        key_refs, val_refs = (k_ref, ck_ref), (v_ref, cv_ref)
    else:
        q_ref, k_ref, v_ref, l1_ref, l2_ref, gs_ref, o_ref = refs
        key_refs, val_refs = (k_ref,), (v_ref,)
    lam = (jnp.exp(jnp.sum(l1_ref[0:1, :] * l1_ref[1:2, :], axis=-1, keepdims=True))
           - jnp.exp(jnp.sum(l2_ref[0:1, :] * l2_ref[1:2, :], axis=-1, keepdims=True)) + lam_init)
    tq = q_ref.shape[1]
    for r0 in range(0, tq, ATTN_ROWS):
        rows = slice(r0, min(r0 + ATTN_ROWS, tq))
        es, fs = [], []
        for mi in range(2):
            q = q_ref[0, rows, mi * dk:(mi + 1) * dk].astype(BF16)
            ss = [_dot_nt(q, kr[0, :, mi * dk:(mi + 1) * dk].astype(BF16)) for kr in key_refs]
            m = functools.reduce(jnp.maximum, [jnp.max(s, axis=-1, keepdims=True) for s in ss])
            e = [jnp.exp2(s - m) for s in ss]
            tot = functools.reduce(jnp.add, [jnp.sum(x, axis=-1, keepdims=True) for x in e])
            es.append(e)
            fs.append((1.0 if mi == 0 else lam) / tot)
        out = None
        for si, vr in enumerate(val_refs):
            w = (es[0][si] * fs[0] - es[1][si] * fs[1]).astype(BF16)
            o = jnp.dot(w, vr[0].astype(BF16), preferred_element_type=F32)
            out = o if out is None else out + o
        o_ref[0, rows, :] = (_rms(out, gs_ref[...]) * (1.0 - lam_init)).astype(o_ref.dtype)


def _diff(q, q_cb0, k, k_cb0, v, v_cb0, ctx, lam_q1, lam_k1, lam_q2, lam_k2, g_subln, lam_init, *, H, dk):
    B, S, _ = q.shape
    dv = 2 * dk
    tq = _tile(S, 2 * ATTN_ROWS, BF16_ROWS)
    l1 = jnp.stack([lam_q1, lam_k1]).astype(F32)
    l2 = jnp.stack([lam_q2, lam_k2]).astype(F32)
    lspec = pl.BlockSpec((2, dk), lambda b, h, n: (0, 0))
    in_specs = [
        pl.BlockSpec((1, tq, dv), lambda b, h, n: (b, n, q_cb0 + h)),
        pl.BlockSpec((1, S, dv), lambda b, h, n: (b, 0, k_cb0 + h)),
        pl.BlockSpec((1, S, dv), lambda b, h, n: (b, 0, v_cb0 + h)),
    ]
    args = [q, k, v]
    if ctx is not None:
        P = ctx[0].shape[1]
        cspec = pl.BlockSpec((1, P, dv), lambda b, h, n: (b, 0, h))
        in_specs += [cspec, cspec]
        args += list(ctx)
    in_specs += [lspec, lspec, pl.BlockSpec((1, dv), lambda b, h, n: (0, 0))]
    args += [l1, l2, g_subln.reshape(1, dv)]
    return pl.pallas_call(
        functools.partial(_diff_kernel, dk=dk, lam_init=lam_init, has_ctx=ctx is not None),
        grid=(B, H, S // tq),
        in_specs=in_specs,
        out_specs=pl.BlockSpec((1, tq, dv), lambda b, h, n: (b, n, h)),
        out_shape=jax.ShapeDtypeStruct((B, S, H * dv), BF16),
        compiler_params=_params("parallel", "parallel", "parallel"),
        name="diff_attention",
    )(*args)


Z_PAD = 8
FFN_ROW_CHUNK = 32
FFN_DOT_ROWS = 1024


def _ffn_up_kernel(h_ref, wg_ref, wv_ref, cwg_ref, cwv_ref, cbg_ref, cbv_ref, o_ref, z_ref, *, tm, tn, S, rc, rm):
    zero = jnp.zeros((Z_PAD, 2 * tn), F32)
    z_ref[0:Z_PAD, :] = zero
    z_ref[Z_PAD + tm:2 * Z_PAD + tm, :] = zero
    for r0 in range(0, tm, rm):
        h = h_ref[r0:r0 + rm, :]
        z_ref[Z_PAD + r0:Z_PAD + r0 + rm, 0:tn] = jnp.dot(h, wg_ref[...], preferred_element_type=F32)
        z_ref[Z_PAD + r0:Z_PAD + r0 + rm, tn:2 * tn] = jnp.dot(h, wv_ref[...], preferred_element_type=F32)

    cw = jnp.concatenate([cwg_ref[...], cwv_ref[...]], axis=1)
    cb = jnp.concatenate([cbg_ref[...], cbv_ref[...]], axis=1)
    span = rc + 2 * Z_PAD
    for r0 in range(0, tm, rc):
        zf = z_ref[r0:r0 + span, :]
        zc = zf[Z_PAD:Z_PAD + rc]
        zp = pltpu.roll(zf, 1, 0)[Z_PAD:Z_PAD + rc]
        zn = pltpu.roll(zf, span - 1, 0)[Z_PAD:Z_PAD + rc]
        if tm > S:
            pos = (r0 + lax.broadcasted_iota(jnp.int32, (rc, 1), 0)) % S
            zp = jnp.where(pos != 0, zp, 0.0)
            zn = jnp.where(pos != S - 1, zn, 0.0)
        y = zp * cw[0:1, :] + zc * cw[1:2, :] + zn * cw[2:3, :] + cb
        g = y[:, 0:tn]
        o_ref[r0:r0 + rc, :] = (g * (1.0 / (1.0 + jnp.exp(-g))) * y[:, tn:2 * tn]).astype(o_ref.dtype)


def _ffn_up(h, S, w_up, conv_w, conv_b, layer):
    M, D = h.shape
    L = w_up.shape[0]
    F = w_up.shape[2] // 2
    tm = _tile(M, max(2048, S), S)
    assert tm % S == 0
    tn = _tile(F, 256, LANES)
    nj = F // tn
    rc = _tile(tm, FFN_ROW_CHUNK, 8)
    rm = _tile(tm, FFN_DOT_ROWS, BF16_ROWS)
    cb = conv_b.reshape(L, 1, 2 * F)
    gate_tile = lambda i, j: (layer, 0, j)
    val_tile = lambda i, j: (layer, 0, nj + j)
    return pl.pallas_call(
        functools.partial(_ffn_up_kernel, tm=tm, tn=tn, S=S, rc=rc, rm=rm),
        grid=(M // tm, nj),
        in_specs=[
            pl.BlockSpec((tm, D), lambda i, j: (i, 0)),
            pl.BlockSpec((None, D, tn), gate_tile),
            pl.BlockSpec((None, D, tn), val_tile),
            pl.BlockSpec((None, 3, tn), gate_tile),
            pl.BlockSpec((None, 3, tn), val_tile),
            pl.BlockSpec((None, 1, tn), gate_tile),
            pl.BlockSpec((None, 1, tn), val_tile),
        ],
        out_specs=pl.BlockSpec((tm, tn), lambda i, j: (i, j)),
        out_shape=jax.ShapeDtypeStruct((M, F), BF16),
        scratch_shapes=[pltpu.VMEM((tm + 2 * Z_PAD, 2 * tn), F32)],
        compiler_params=_params("parallel", "parallel"),
        name="ffn_up_conv_gate",
    )(h, w_up, w_up, conv_w, conv_w, cb, cb)


def kernel(x_prompt, x_sample, cache_a_k, cache_a_v, cache_c_ckv, cache_c_krope, cache_d_k, cache_d_v, c, c_ctx, w_mod, b_mod, g_pre_mix, g_post_mix, g_pre_ffn, g_post_ffn, w_in_even, w_out_even, a_sink, w_pool, pool_scale, w_in_odd, w_out_odd, g_q_norm, w_uq, g_kv_norm, w_uk, w_uv, lambda_q1, lambda_k1, lambda_q2, lambda_k2, g_subln, w_up, conv_w, conv_b, w_down):
    B1, S1, D = x_prompt.shape
    B2, S2, _ = x_sample.shape
    depth = w_mod.shape[0]
    P = cache_a_k.shape[2]
    KV, hd = cache_a_k.shape[3], cache_a_k.shape[4]
    AH = a_sink.shape[1]
    G = AH // KV
    NG, Cg = w_pool.shape[1], w_pool.shape[2]
    qa, kva, pool_dim = AH * hd, KV * hd, NG * Cg
    CQ = g_q_norm.shape[1]
    CKV, CH, CN = w_uk.shape[1], w_uk.shape[2], w_uk.shape[3]
    CR = cache_c_krope.shape[3]
    CV = w_uv.shape[3]
    DH, DK = cache_d_k.shape[3], cache_d_k.shape[5]
    DV = 2 * DK
    dqk = DH * 2 * DK
    assert hd == LANES and CN == LANES and CV == CN and 2 * CR == LANES and DK == LANES

    R = -(-(B2 + 1) // 8) * 8
    cond = jnp.zeros((R, D), F32).at[:B2].set(c).at[B2].set(c_ctx)
    mods = _mods(cond, w_mod, b_mod).reshape(depth, R, 1, 6 * D)
    SH_M, SC_M, G_M, SH_F, SC_F, G_F = range(6)

    groups = [
        dict(x=x_prompt, B=B1, S=S1, row0=B2, stride=0, latent=False),
        dict(x=x_sample, B=B2, S=S2, row0=0, stride=1, latent=True),
    ]
    rope_hd = _rope_tables(S2, hd)
    rope_cr = _rope_tables(S2, CR)
    w_up_b = w_up.astype(BF16)
    w_down_b = w_down.astype(BF16)

    new_state = {}
    for grp in groups:
        B, S, latent = grp["B"], grp["S"], grp["latent"]
        M = B * S
        row0, stride = grp["row0"], grp["stride"]
        x = grp["x"]
        (h,) = _resid_norm(x, row0, stride, norm=(g_pre_mix[0], mods[0], SC_M, SH_M))
        for i in range(depth):
            j = i // 2
            hf = h.reshape(M, D)
            pdt = BF16 if latent else F32

            def rope(tables, pair, col0, col1):
                return (tables, pair, col0 // LANES, col1 // LANES, S) if latent else None

            if i % 2 == 0:
                wi = w_in_even[j]
                w_in = jnp.concatenate([wi[:, :qa] * _logit_scale(hd), wi[:, qa:]], axis=1).astype(BF16)
                proj = _matmul(hf, w_in, out_dtype=pdt, rope=rope(rope_hd, hd // 4, 0, qa + kva)).reshape(B, S, -1)
                if latent:
                    ctx = (cache_a_k[:, j].reshape(B, P, kva), cache_a_v[:, j].reshape(B, P, kva))
                else:
                    ctx = None
                    new_state["a_k"] = proj[:, :, qa:qa + kva].reshape(B, 1, S, KV, hd)
                    new_state["a_v"] = proj[:, :, qa + kva:qa + 2 * kva].reshape(B, 1, S, KV, hd)
                a = _gqa(proj, 0, proj, qa // hd, proj, (qa + kva) // hd, a_sink[j], ctx, KV=KV, G=G, hd=hd)
                bmix = _pool(proj, (qa + 2 * kva) // Cg, w_pool[j], pool_scale[j])
                y = _matmul_parts([a.reshape(M, qa), bmix.reshape(M, pool_dim)], w_out_even[j].astype(BF16))
            else:
                lam_init = 0.8 - 0.6 * math.exp(-0.3 * i)
                wi = w_in_odd[j]
                c1, c2 = CQ + CKV, CQ + CKV + CR
                w_main = jnp.concatenate([wi[:, :c1], wi[:, c2:c2 + dqk] * _logit_scale(DK), wi[:, c2 + dqk:]],
                                         axis=1).astype(BF16)
                w_kr = jnp.concatenate([wi[:, c1:c2], jnp.zeros((D, LANES - CR), F32)], axis=1).astype(BF16)
                o_dq, o_dk, o_dv = c1, c1 + dqk, c1 + 2 * dqk
                proj = _matmul(hf, w_main, out_dtype=pdt, rope=rope(rope_hd, hd // 4, o_dq, o_dv))
                krp = _matmul(hf, w_kr, out_dtype=pdt, tn_pref=LANES,
                              rope=rope(rope_cr, CR // 4, 0, LANES)).reshape(B, S, LANES)
                (cqn,) = _rms_cols(proj, 0, g_q_norm[j], [BF16])
                ckvn_f, ckvn_b = _rms_cols(proj, CQ // CKV, g_kv_norm[j], [F32, BF16])
                wq = w_uq[j].reshape(CQ, CH, CN + CR)
                wq = jnp.concatenate([wq[:, :, :CN].reshape(CQ, CH * CN), wq[:, :, CN:].reshape(CQ, CH * CR)], axis=1)
                wq = wq * _logit_scale(CN + CR)
                qall = _matmul(cqn, wq.astype(BF16), out_dtype=pdt,
                               rope=rope(rope_cr, CR // 4, CH * CN, CH * (CN + CR))).reshape(B, S, CH * (CN + CR))
                wkv = jnp.concatenate([w_uk[j].reshape(CKV, CH * CN), w_uv[j].reshape(CKV, CH * CV)], axis=1).astype(BF16)
                proj3 = proj.reshape(B, S, -1)
                if latent:
                    ckv_all = jnp.concatenate([ckvn_b.reshape(B, S, CKV), cache_c_ckv[:, j].astype(BF16)], axis=1)
                    kr_all = jnp.concatenate([krp[:, :, :CR], cache_c_krope[:, j].astype(BF16)], axis=1)
                    d_ctx = (cache_d_k[:, j].reshape(B, P, dqk), cache_d_v[:, j].reshape(B, P, DH * DV))
                else:
                    ckv_all = ckvn_b.reshape(B, S, CKV)
                    kr_all = krp[:, :, :CR].astype(BF16)
                    d_ctx = None
                    new_state["c_ckv"] = ckvn_f.reshape(B, 1, S, CKV)
                    new_state["c_krope"] = krp[:, :, :CR].reshape(B, 1, S, CR)
                    new_state["d_k"] = proj3[:, :, o_dk:o_dk + dqk].reshape(B, 1, S, DH, 2, DK)
                    new_state["d_v"] = proj3[:, :, o_dv:o_dv + dqk].reshape(B, 1, S, DH, DV)
                Sk = ckv_all.shape[1]
                knv = _matmul(ckv_all.reshape(B * Sk, CKV), wkv, out_dtype=BF16).reshape(B, Sk, -1)
                zk = jnp.zeros_like(kr_all)
                kr2 = jnp.concatenate([kr_all, zk, zk, kr_all], axis=-1)
                c_out = _mla(qall, 0, qall, CH * CN // LANES, knv, kr2, H=CH, dn=CN, dr=CR)
                d_out = _diff(proj3, o_dq // DV, proj3, o_dk // DV, proj3, o_dv // DV, d_ctx,
                              lambda_q1[j], lambda_k1[j], lambda_q2[j], lambda_k2[j], g_subln[j], lam_init, H=DH, dk=DK)
                y = _matmul_parts([c_out.reshape(M, -1), d_out.reshape(M, -1)], w_out_odd[j].astype(BF16))
            x, h = _resid_norm(x, row0, stride, resid=(y, g_post_mix[i], mods[i], G_M),
                               norm=(g_pre_ffn[i], mods[i], SC_F, SH_F))
            hmid = _ffn_up(h.reshape(M, D), S, w_up_b, conv_w, conv_b, i)
            f = _matmul(hmid, w_down_b, layer=i, tm_pref=512, tn_pref=512)
            if i + 1 < depth:
                x, h = _resid_norm(x, row0, stride, resid=(f, g_post_ffn[i], mods[i], G_F),
                                   norm=(g_pre_mix[i + 1], mods[i + 1], SC_M, SH_M))
            else:
                (x,) = _resid_norm(x, row0, stride, resid=(f, g_post_ffn[i], mods[i], G_F))
        grp["out"] = x

    return (groups[0]["out"], groups[1]["out"], new_state["a_k"], new_state["a_v"], new_state["c_ckv"],
            new_state["c_krope"], new_state["d_k"], new_state["d_v"])
```

```python
import functools
import math

import jax
import jax.numpy as jnp
from jax import lax
from jax.experimental import pallas as pl
from jax.experimental.pallas import tpu as pltpu

GRID_W = 64
Q_BLOCK = 128
WINDOW = 128
ROPE_BASE = 10000.0
EPS = 1e-6
NEG_INF = -1e30
POOL_WINDOWS = (2, 4, 8, 16)

LANES = 128
BF16_ROWS = 16
VMEM_LIMIT = 56 * 1024 * 1024

BF16 = jnp.bfloat16
F32 = jnp.float32


def _tile(dim, pref, align):
    best = None
    t = align
    while t <= min(dim, pref):
        if dim % t == 0:
            best = t
        t += align
    return dim if best is None else best


def _params(*sem):
    return pltpu.CompilerParams(dimension_semantics=sem, vmem_limit_bytes=VMEM_LIMIT)


def _mods_kernel(c_ref, w_ref, b_ref, o_ref):
    c = c_ref[...]
    s = (c * (1.0 / (1.0 + jnp.exp(-c)))).astype(BF16)
    o_ref[...] = jnp.dot(s, w_ref[...].astype(BF16), preferred_element_type=F32) + b_ref[...]


def _mods(cond, w_mod, b_mod):
    depth, D, N = w_mod.shape
    R = cond.shape[0]
    tn = _tile(N, 512, LANES)
    return pl.pallas_call(
        _mods_kernel,
        grid=(depth, N // tn),
        in_specs=[
            pl.BlockSpec((R, D), lambda l, j: (0, 0)),
            pl.BlockSpec((None, D, tn), lambda l, j: (l, 0, j)),
            pl.BlockSpec((None, 1, tn), lambda l, j: (l, 0, j)),
        ],
        out_specs=pl.BlockSpec((None, R, tn), lambda l, j: (l, 0, j)),
        out_shape=jax.ShapeDtypeStruct((depth, R, N), F32),
        compiler_params=_params("parallel", "parallel"),
        name="mods",
    )(cond, w_mod, b_mod.reshape(depth, 1, N))


def _rms(x, g):
    return x * lax.rsqrt(jnp.mean(x * x, axis=-1, keepdims=True) + EPS) * g


def _resid_norm_kernel(*refs, has_resid, has_norm):
    refs = list(refs)
    x = refs.pop(0)[0]
    if has_resid:
        y_ref, gpost_ref, gate_ref = refs[:3]
        refs = refs[3:]
        x = x + gate_ref[...] * _rms(y_ref[0].astype(F32), gpost_ref[...])
    if has_norm:
        gpre_ref, sc_ref, sh_ref = refs[:3]
        refs = refs[3:]
    if has_resid:
        refs.pop(0)[0] = x
    if has_norm:
        h = _rms(x, gpre_ref[...]) * (1.0 + sc_ref[...]) + sh_ref[...]
        refs.pop(0)[0] = h.astype(BF16)


def _resid_norm(x, row0, row_stride, *, resid=None, norm=None):
    B, S, D = x.shape
    ts = _tile(S, 256, 8)
    xspec = pl.BlockSpec((1, ts, D), lambda b, s: (b, s, 0))
    gspec = pl.BlockSpec((1, D), lambda b, s: (0, 0))

    def mspec(col):
        return pl.BlockSpec((None, 1, D), lambda b, s: (row0 + b * row_stride, 0, col))

    args, in_specs, out_shape, out_specs = [x], [xspec], [], []
    if resid is not None:
        y, gpost, mods, gate_col = resid
        args += [y.reshape(B, S, D), gpost.reshape(1, D), mods]
        in_specs += [xspec, gspec, mspec(gate_col)]
        out_shape.append(jax.ShapeDtypeStruct((B, S, D), F32))
        out_specs.append(xspec)
    if norm is not None:
        gpre, mods, sc_col, sh_col = norm
        args += [gpre.reshape(1, D), mods, mods]
        in_specs += [gspec, mspec(sc_col), mspec(sh_col)]
        out_shape.append(jax.ShapeDtypeStruct((B, S, D), BF16))
        out_specs.append(xspec)
    outs = pl.pallas_call(
        functools.partial(_resid_norm_kernel, has_resid=resid is not None, has_norm=norm is not None),
        grid=(B, S // ts),
        in_specs=in_specs,
        out_specs=out_specs,
        out_shape=out_shape,
        compiler_params=_params("parallel", "parallel"),
        name="resid_norm",
    )(*args)
    return outs


def _mm_kernel(x_ref, w_ref, o_ref):
    o_ref[...] = jnp.dot(x_ref[...].astype(BF16), w_ref[...], preferred_element_type=F32).astype(o_ref.dtype)


def _mm_parts_kernel(*refs):
    o_ref = refs[-1]
    acc = None
    for x_ref, w_ref in zip(refs[0:-1:2], refs[1:-1:2]):
        d = jnp.dot(x_ref[...], w_ref[...], preferred_element_type=F32)
        acc = d if acc is None else acc + d
    o_ref[...] = acc.astype(o_ref.dtype)


def _matmul_parts(xs, w, *, out_dtype=F32, tm_pref=1024, tn_pref=512):
    M = xs[0].shape[0]
    N = w.shape[1]
    starts = [sum(x.shape[1] for x in xs[:n]) for n in range(len(xs))]
    if any(k0 % x.shape[1] for k0, x in zip(starts, xs)):
        return _matmul(jnp.concatenate(xs, axis=1), w, out_dtype=out_dtype, tm_pref=tm_pref, tn_pref=tn_pref)
    tm = _tile(M, tm_pref, BF16_ROWS)
    tn = _tile(N, tn_pref, LANES)
    in_specs, args, k0 = [], [], 0
    for x in xs:
        kp = x.shape[1]
        in_specs += [pl.BlockSpec((tm, kp), lambda i, j: (i, 0)),
                     pl.BlockSpec((kp, tn), lambda i, j, kb=k0 // kp: (kb, j))]
        args += [x, w]
        k0 += kp
    return pl.pallas_call(
        _mm_parts_kernel,
        grid=(M // tm, N // tn),
        in_specs=in_specs,
        out_specs=pl.BlockSpec((tm, tn), lambda i, j: (i, j)),
        out_shape=jax.ShapeDtypeStruct((M, N), out_dtype),
        compiler_params=_params("parallel", "parallel"),
        name="matmul_parts",
    )(*args)


def _rotate_slab(x, cos, sin, first, pair):
    partner = jnp.where(first, pltpu.roll(x, LANES - pair, 1), pltpu.roll(x, pair, 1))
    return x * cos + partner * sin


def _mm_rope_kernel(x_ref, w_ref, cos_ref, sin_ref, o_ref, *, pair, g0, g1):
    tm, tn = o_ref.shape
    spt = tn // LANES
    acc = jnp.dot(x_ref[...].astype(BF16), w_ref[...], preferred_element_type=F32)
    j = pl.program_id(1)
    lane = lax.broadcasted_iota(jnp.int32, (tm, LANES), 1)
    first = (lane % (2 * pair)) < pair

    def emit(rotated):
        for c in range(spt):
            x = acc[:, c * LANES:(c + 1) * LANES]
            if rotated[c]:
                x = _rotate_slab(x, cos_ref[...], sin_ref[...], first, pair)
            o_ref[:, c * LANES:(c + 1) * LANES] = x.astype(o_ref.dtype)

    full_lo, full_hi = -(-g0 // spt), g1 // spt
    mixed = [t for t in {g0 // spt, (g1 - 1) // spt} if not full_lo <= t < full_hi]
    is_full = (j >= full_lo) & (j < full_hi)
    is_mixed = functools.reduce(jnp.logical_or, [j == t for t in mixed], jnp.bool_(False))
    pl.when(is_full)(lambda: emit([True] * spt))
    for t in mixed:
        pl.when(j == t)(lambda t=t: emit([g0 <= t * spt + c < g1 for c in range(spt)]))
    pl.when(jnp.logical_not(is_full | is_mixed))(lambda: emit([False] * spt))


def _matmul(x, w, *, layer=None, out_dtype=F32, tm_pref=1024, tn_pref=512, rope=None):
    M, K = x.shape
    N = w.shape[-1]
    tn = _tile(N, tn_pref, LANES)
    if layer is None:
        wspec = pl.BlockSpec((K, tn), lambda i, j: (0, j))
    else:
        wspec = pl.BlockSpec((None, K, tn), lambda i, j: (layer, 0, j))
    if rope is None:
        tm = _tile(M, tm_pref, BF16_ROWS)
        kern, extra_specs, extra_args = _mm_kernel, [], []
    else:
        (cos, sin), pair, g0, g1, S = rope
        tm = _tile(S, tm_pref, BF16_ROWS)
        tps = S // tm
        tspec = pl.BlockSpec((tm, LANES), lambda i, j: (i % tps, 0))
        kern = functools.partial(_mm_rope_kernel, pair=pair, g0=g0, g1=g1)
        extra_specs, extra_args = [tspec, tspec], [cos, sin]
    return pl.pallas_call(
        kern,
        grid=(M // tm, N // tn),
        in_specs=[
            pl.BlockSpec((tm, K), lambda i, j: (i, 0)),
            wspec,
        ] + extra_specs,
        out_specs=pl.BlockSpec((tm, tn), lambda i, j: (i, j)),
        out_shape=jax.ShapeDtypeStruct((M, N), out_dtype),
        compiler_params=_params("parallel", "parallel"),
        name="matmul",
    )(x, w, *extra_args)


def _rms_cols_kernel(x_ref, g_ref, *o_refs):
    y = _rms(x_ref[...].astype(F32), g_ref[...])
    for o_ref in o_refs:
        o_ref[...] = y.astype(o_ref.dtype)


def _rms_cols(x, col_block, g, out_dtypes):
    M = x.shape[0]
    W = g.shape[-1]
    tm = _tile(M, 1024, BF16_ROWS)
    spec = pl.BlockSpec((tm, W), lambda i: (i, 0))
    return pl.pallas_call(
        _rms_cols_kernel,
        grid=(M // tm,),
        in_specs=[pl.BlockSpec((tm, W), lambda i: (i, col_block)), pl.BlockSpec((1, W), lambda i: (0, 0))],
        out_specs=[spec] * len(out_dtypes),
        out_shape=[jax.ShapeDtypeStruct((M, W), dt) for dt in out_dtypes],
        compiler_params=_params("parallel"),
        name="rms_cols",
    )(x, g.reshape(1, W))


def _rope_tables(S, width):
    d = width // 2
    half = d // 2
    inv = ROPE_BASE ** (-jnp.arange(half, dtype=F32) / half)
    t = jnp.arange(S)
    row = (t // GRID_W).astype(F32)[:, None] * inv[None, :]
    col = (t % GRID_W).astype(F32)[:, None] * inv[None, :]
    cos = jnp.concatenate([jnp.cos(row), jnp.cos(row), jnp.cos(col), jnp.cos(col)], axis=-1)
    sin = jnp.concatenate([-jnp.sin(row), jnp.sin(row), -jnp.sin(col), jnp.sin(col)], axis=-1)
    reps = LANES // width
    return jnp.tile(cos, (1, reps)), jnp.tile(sin, (1, reps))


LOG2E = math.log2(math.e)
ATTN_ROWS = 256


def _logit_scale(d):
    return d ** -0.5 * LOG2E


def _dot_nt(a, b):
    return lax.dot_general(a, b, (((1,), (1,)), ((), ())), preferred_element_type=F32)


def _gqa_kernel(*refs, G, hd, tq, nb, S, windowed):
    if windowed:
        q_ref, k_ref, v_ref, sink_ref, ck_ref, cv_ref, o_ref = refs
    else:
        q_ref, k_ref, v_ref, sink_ref, o_ref = refs
    rows = G * tq
    sink = jnp.concatenate([jnp.broadcast_to(sink_ref[0, g:g + 1, :], (tq, LANES)) for g in range(G)], axis=0) * LOG2E
    for blk in range(nb):
        r0 = blk * tq
        q = jnp.concatenate([q_ref[0, r0:r0 + tq, g * hd:(g + 1) * hd] for g in range(G)], axis=0).astype(BF16)
        if windowed:
            n = pl.program_id(2) * nb + blk
            span = 3 * Q_BLOCK
            start = pl.multiple_of(jnp.clip(n * tq - Q_BLOCK, 0, S - span), Q_BLOCK)
            kw = k_ref[0, pl.ds(start, span), :].astype(BF16)
            vw = v_ref[0, pl.ds(start, span), :].astype(BF16)
            qpos = n * tq + lax.broadcasted_iota(jnp.int32, (rows, span), 0) % tq
            kpos = start + lax.broadcasted_iota(jnp.int32, (rows, span), 1)
            s = jnp.where(jnp.abs(kpos - qpos) <= WINDOW, _dot_nt(q, kw), NEG_INF)
            sc = _dot_nt(q, ck_ref[0].astype(BF16))
            m = jnp.maximum(jnp.max(s, axis=-1, keepdims=True), jnp.max(sc, axis=-1, keepdims=True))
        else:
            kw = k_ref[0].astype(BF16)
            vw = v_ref[0].astype(BF16)
            s = _dot_nt(q, kw)
            m = jnp.max(s, axis=-1, keepdims=True)
        p = jnp.exp2(s - m)
        l = jnp.sum(p, axis=-1, keepdims=True)
        acc = jnp.dot(p.astype(BF16), vw, preferred_element_type=F32)
        if windowed:
            pc = jnp.exp2(sc - m)
            l = l + jnp.sum(pc, axis=-1, keepdims=True)
            acc = acc + jnp.dot(pc.astype(BF16), cv_ref[0].astype(BF16), preferred_element_type=F32)
        out = acc / (l + jnp.exp2(sink - m))
        for g in range(G):
            o_ref[0, r0:r0 + tq, g * hd:(g + 1) * hd] = out[g * tq:(g + 1) * tq].astype(o_ref.dtype)


def _gqa(q, q_cb0, k, k_cb0, v, v_cb0, sink, ctx, *, KV, G, hd):
    B, S, _ = q.shape
    windowed = ctx is not None
    tq = Q_BLOCK if windowed else _tile(S, 256, BF16_ROWS)
    nb = max(n for n in (1, 2, 4) if S % (n * tq) == 0) if windowed else 1
    ts = nb * tq
    sink3 = jnp.broadcast_to(sink.astype(F32).reshape(KV, G, 1), (KV, G, LANES))
    in_specs = [
        pl.BlockSpec((1, ts, G * hd), lambda b, h, n: (b, n, q_cb0 + h)),
        pl.BlockSpec((1, S, hd), lambda b, h, n: (b, 0, k_cb0 + h)),
        pl.BlockSpec((1, S, hd), lambda b, h, n: (b, 0, v_cb0 + h)),
        pl.BlockSpec((1, G, LANES), lambda b, h, n: (h, 0, 0)),
    ]
    args = [q, k, v, sink3]
    if windowed:
        ck, cv = ctx
        P = ck.shape[1]
        cspec = pl.BlockSpec((1, P, hd), lambda b, h, n: (b, 0, h))
        in_specs += [cspec, cspec]
        args += [ck, cv]
    return pl.pallas_call(
        functools.partial(_gqa_kernel, G=G, hd=hd, tq=tq, nb=nb, S=S, windowed=windowed),
        grid=(B, KV, S // ts),
        in_specs=in_specs,
        out_specs=pl.BlockSpec((1, ts, G * hd), lambda b, h, n: (b, n, h)),
        out_shape=jax.ShapeDtypeStruct((B, S, KV * G * hd), BF16),
        compiler_params=_params("parallel", "parallel", "parallel"),
        name="gqa_attention",
    )(*args)


POOL_PAD = 16


def _pool_kernel(u_ref, w_ref, ps_ref, o_ref, pad_ref, *, S):
    grp = pl.program_id(1)
    u = u_ref[0].astype(F32)
    C = u.shape[-1]
    pad_ref[0:POOL_PAD, :] = jnp.zeros((POOL_PAD, C), F32)
    pad_ref[POOL_PAD + S:2 * POOL_PAD + S, :] = jnp.zeros((POOL_PAD, C), F32)
    pad_ref[POOL_PAD:POOL_PAD + S, :] = u
    t = lax.broadcasted_iota(jnp.int32, (S, 1), 0)
    for gi, win in enumerate(POOL_WINDOWS):
        @pl.when(grp == gi)
        def _(win=win):
            half = win // 2
            total = pad_ref[POOL_PAD - half:POOL_PAD - half + S, :]
            for off in range(-half + 1, half):
                total = total + pad_ref[POOL_PAD + off:POOL_PAD + off + S, :]
            count = (jnp.clip(t + half, 0, S) - jnp.clip(t - half, 0, S)).astype(F32)
            pooled = (total / count - u).astype(BF16)
            y = jnp.dot(pooled, w_ref[0].astype(BF16), preferred_element_type=F32) * ps_ref[...]
            o_ref[0] = y.astype(o_ref.dtype)


def _pool(u, u_cb0, w_pool, pool_scale):
    B, S, _ = u.shape
    NG, Cg, _ = w_pool.shape
    assert NG == len(POOL_WINDOWS)
    return pl.pallas_call(
        functools.partial(_pool_kernel, S=S),
        grid=(B, NG),
        in_specs=[
            pl.BlockSpec((1, S, Cg), lambda b, g: (b, 0, u_cb0 + g)),
            pl.BlockSpec((1, Cg, Cg), lambda b, g: (g, 0, 0)),
            pl.BlockSpec((1, Cg), lambda b, g: (0, g)),
        ],
        out_specs=pl.BlockSpec((1, S, Cg), lambda b, g: (b, 0, g)),
        out_shape=jax.ShapeDtypeStruct((B, S, NG * Cg), BF16),
        scratch_shapes=[pltpu.VMEM((S + 2 * POOL_PAD, Cg), F32)],
        compiler_params=_params("parallel", "parallel"),
        name="pool_mixer",
    )(u, w_pool, pool_scale.reshape(1, NG * Cg))


def _mla_kernel(qn_ref, qr_ref, kn_ref, kr_ref, v_ref, o_ref, *, dn):
    tq = qn_ref.shape[1]
    ones = jnp.ones((kn_ref.shape[1], LANES), BF16)
    for r0 in range(0, tq, ATTN_ROWS):
        rows = slice(r0, min(r0 + ATTN_ROWS, tq))
        qr = qr_ref[0, rows, :].astype(BF16)
        for e in range(2):
            q = jnp.concatenate([qn_ref[0, rows, e * dn:(e + 1) * dn].astype(BF16), qr], axis=-1)
            k = jnp.concatenate([kn_ref[0, :, e * dn:(e + 1) * dn], kr_ref[0, :, e * LANES:(e + 1) * LANES]], axis=-1)
            s = _dot_nt(q, k)
            p = jnp.exp2(s - jnp.max(s, axis=-1, keepdims=True)).astype(BF16)
            acc = jnp.dot(p, jnp.concatenate([v_ref[0, :, e * dn:(e + 1) * dn], ones], axis=-1),
                          preferred_element_type=F32)
            o_ref[0, rows, e * dn:(e + 1) * dn] = (acc[:, 0:dn] / acc[:, dn:2 * dn]).astype(o_ref.dtype)


def _mla(qn, qn_cb0, qr, qr_cb0, knv, kr2, *, H, dn, dr):
    B, S, _ = qn.shape
    Sk = knv.shape[1]
    assert dn == LANES and 2 * dr == LANES
    tq = _tile(S, 2 * ATTN_ROWS, BF16_ROWS)
    HP = H // 2
    return pl.pallas_call(
        functools.partial(_mla_kernel, dn=dn),
        grid=(B, HP, S // tq),
        in_specs=[
            pl.BlockSpec((1, tq, 2 * dn), lambda b, h, n: (b, n, qn_cb0 + h)),
            pl.BlockSpec((1, tq, LANES), lambda b, h, n: (b, n, qr_cb0 + h)),
            pl.BlockSpec((1, Sk, 2 * dn), lambda b, h, n: (b, 0, h)),
            pl.BlockSpec((1, Sk, 2 * LANES), lambda b, h, n: (b, 0, 0)),
            pl.BlockSpec((1, Sk, 2 * dn), lambda b, h, n: (b, 0, HP + h)),
        ],
        out_specs=pl.BlockSpec((1, tq, 2 * dn), lambda b, h, n: (b, n, h)),
        out_shape=jax.ShapeDtypeStruct((B, S, H * dn), BF16),
        compiler_params=_params("parallel", "parallel", "parallel"),
        name="mla_attention",
    )(qn, qr, knv, kr2, knv)


def _diff_kernel(*refs, dk, lam_init, has_ctx):
    if has_ctx:
        q_ref, k_ref, v_ref, ck_ref, cv_ref, l1_ref, l2_ref, gs_ref, o_ref = refs
        key_refs, val_refs = (k_ref, ck_ref), (v_ref, cv_ref)
    else:
        q_ref, k_ref, v_ref, l1_ref, l2_ref, gs_ref, o_ref = refs
        key_refs, val_refs = (k_ref,), (v_ref,)
    lam = (jnp.exp(jnp.sum(l1_ref[0:1, :] * l1_ref[1:2, :], axis=-1, keepdims=True))
           - jnp.exp(jnp.sum(l2_ref[0:1, :] * l2_ref[1:2, :], axis=-1, keepdims=True)) + lam_init)
    tq = q_ref.shape[1]
    for r0 in range(0, tq, ATTN_ROWS):
        rows = slice(r0, min(r0 + ATTN_ROWS, tq))
        out = None
        for mi in range(2):
            q = q_ref[0, rows, mi * dk:(mi + 1) * dk].astype(BF16)
            ss = [_dot_nt(q, kr[0, :, mi * dk:(mi + 1) * dk].astype(BF16)) for kr in key_refs]
            m = functools.reduce(jnp.maximum, [jnp.max(s, axis=-1, keepdims=True) for s in ss])
            es = [jnp.exp2(s - m) for s in ss]
            tot = functools.reduce(jnp.add, [jnp.sum(e, axis=-1, keepdims=True) for e in es])
            pv = functools.reduce(jnp.add, [jnp.dot(e.astype(BF16), vr[0].astype(BF16), preferred_element_type=F32)
                                            for e, vr in zip(es, val_refs)])
            out = pv / tot if mi == 0 else out - pv * (lam / tot)
        o_ref[0, rows, :] = (_rms(out, gs_ref[...]) * (1.0 - lam_init)).astype(o_ref.dtype)


def _diff(q, q_cb0, k, k_cb0, v, v_cb0, ctx, lam_q1, lam_k1, lam_q2, lam_k2, g_subln, lam_init, *, H, dk):
    B, S, _ = q.shape
    dv = 2 * dk
    tq = _tile(S, 2 * ATTN_ROWS, BF16_ROWS)
    l1 = jnp.stack([lam_q1, lam_k1]).astype(F32)
    l2 = jnp.stack([lam_q2, lam_k2]).astype(F32)
    lspec = pl.BlockSpec((2, dk), lambda b, h, n: (0, 0))
    in_specs = [
        pl.BlockSpec((1, tq, dv), lambda b, h, n: (b, n, q_cb0 + h)),
        pl.BlockSpec((1, S, dv), lambda b, h, n: (b, 0, k_cb0 + h)),
        pl.BlockSpec((1, S, dv), lambda b, h, n: (b, 0, v_cb0 + h)),
    ]
    args = [q, k, v]
    if ctx is not None:
        P = ctx[0].shape[1]
        cspec = pl.BlockSpec((1, P, dv), lambda b, h, n: (b, 0, h))
        in_specs += [cspec, cspec]
        args += list(ctx)
    in_specs += [lspec, lspec, pl.BlockSpec((1, dv), lambda b, h, n: (0, 0))]
    args += [l1, l2, g_subln.reshape(1, dv)]
    return pl.pallas_call(
        functools.partial(_diff_kernel, dk=dk, lam_init=lam_init, has_ctx=ctx is not None),
        grid=(B, H, S // tq),
        in_specs=in_specs,
        out_specs=pl.BlockSpec((1, tq, dv), lambda b, h, n: (b, n, h)),
        out_shape=jax.ShapeDtypeStruct((B, S, H * dv), BF16),
        compiler_params=_params("parallel", "parallel", "parallel"),
        name="diff_attention",
    )(*args)


Z_PAD = 8
FFN_ROW_CHUNK = 32
FFN_DOT_ROWS = 1024


def _ffn_up_kernel(h_ref, wg_ref, wv_ref, cwg_ref, cwv_ref, cbg_ref, cbv_ref, o_ref, z_ref, *, tm, tn, S, rc, rm):
    zero = jnp.zeros((Z_PAD, 2 * tn), F32)
    z_ref[0:Z_PAD, :] = zero
    z_ref[Z_PAD + tm:2 * Z_PAD + tm, :] = zero
    for r0 in range(0, tm, rm):
        h = h_ref[r0:r0 + rm, :]
        z_ref[Z_PAD + r0:Z_PAD + r0 + rm, 0:tn] = jnp.dot(h, wg_ref[...], preferred_element_type=F32)
        z_ref[Z_PAD + r0:Z_PAD + r0 + rm, tn:2 * tn] = jnp.dot(h, wv_ref[...], preferred_element_type=F32)

    cw = jnp.concatenate([cwg_ref[...], cwv_ref[...]], axis=1)
    cb = jnp.concatenate([cbg_ref[...], cbv_ref[...]], axis=1)
    span = rc + 2 * Z_PAD
    for r0 in range(0, tm, rc):
        zf = z_ref[r0:r0 + span, :]
        zc = zf[Z_PAD:Z_PAD + rc]
        zp = pltpu.roll(zf, 1, 0)[Z_PAD:Z_PAD + rc]
        zn = pltpu.roll(zf, span - 1, 0)[Z_PAD:Z_PAD + rc]
        if tm > S:
            pos = (r0 + lax.broadcasted_iota(jnp.int32, (rc, 1), 0)) % S
            zp = jnp.where(pos != 0, zp, 0.0)
            zn = jnp.where(pos != S - 1, zn, 0.0)
        y = zp * cw[0:1, :] + zc * cw[1:2, :] + zn * cw[2:3, :] + cb
        g = y[:, 0:tn]
        o_ref[r0:r0 + rc, :] = (g * (1.0 / (1.0 + jnp.exp(-g))) * y[:, tn:2 * tn]).astype(o_ref.dtype)


def _ffn_up(h, S, w_up, conv_w, conv_b, layer):
    M, D = h.shape
    L = w_up.shape[0]
    F = w_up.shape[2] // 2
    tm = _tile(M, max(2048, S), S)
    assert tm % S == 0
    tn = _tile(F, 256, LANES)
    nj = F // tn
    rc = _tile(tm, FFN_ROW_CHUNK, 8)
    rm = _tile(tm, FFN_DOT_ROWS, BF16_ROWS)
    cb = conv_b.reshape(L, 1, 2 * F)
    gate_tile = lambda i, j: (layer, 0, j)
    val_tile = lambda i, j: (layer, 0, nj + j)
    return pl.pallas_call(
        functools.partial(_ffn_up_kernel, tm=tm, tn=tn, S=S, rc=rc, rm=rm),
        grid=(M // tm, nj),
        in_specs=[
            pl.BlockSpec((tm, D), lambda i, j: (i, 0)),
            pl.BlockSpec((None, D, tn), gate_tile),
            pl.BlockSpec((None, D, tn), val_tile),
            pl.BlockSpec((None, 3, tn), gate_tile),
            pl.BlockSpec((None, 3, tn), val_tile),
            pl.BlockSpec((None, 1, tn), gate_tile),
            pl.BlockSpec((None, 1, tn), val_tile),
        ],
        out_specs=pl.BlockSpec((tm, tn), lambda i, j: (i, j)),
        out_shape=jax.ShapeDtypeStruct((M, F), BF16),
        scratch_shapes=[pltpu.VMEM((tm + 2 * Z_PAD, 2 * tn), F32)],
        compiler_params=_params("parallel", "parallel"),
        name="ffn_up_conv_gate",
    )(h, w_up, w_up, conv_w, conv_w, cb, cb)


def kernel(x_prompt, x_sample, cache_a_k, cache_a_v, cache_c_ckv, cache_c_krope, cache_d_k, cache_d_v, c, c_ctx, w_mod, b_mod, g_pre_mix, g_post_mix, g_pre_ffn, g_post_ffn, w_in_even, w_out_even, a_sink, w_pool, pool_scale, w_in_odd, w_out_odd, g_q_norm, w_uq, g_kv_norm, w_uk, w_uv, lambda_q1, lambda_k1, lambda_q2, lambda_k2, g_subln, w_up, conv_w, conv_b, w_down):
    B1, S1, D = x_prompt.shape
    B2, S2, _ = x_sample.shape
    depth = w_mod.shape[0]
    P = cache_a_k.shape[2]
    KV, hd = cache_a_k.shape[3], cache_a_k.shape[4]
    AH = a_sink.shape[1]
    G = AH // KV
    NG, Cg = w_pool.shape[1], w_pool.shape[2]
    qa, kva, pool_dim = AH * hd, KV * hd, NG * Cg
    CQ = g_q_norm.shape[1]
    CKV, CH, CN = w_uk.shape[1], w_uk.shape[2], w_uk.shape[3]
    CR = cache_c_krope.shape[3]
    CV = w_uv.shape[3]
    DH, DK = cache_d_k.shape[3], cache_d_k.shape[5]
    DV = 2 * DK
    dqk = DH * 2 * DK
    assert hd == LANES and CN == LANES and CV == CN and 2 * CR == LANES and DK == LANES

    R = -(-(B2 + 1) // 8) * 8
    cond = jnp.zeros((R, D), F32).at[:B2].set(c).at[B2].set(c_ctx)
    mods = _mods(cond, w_mod, b_mod).reshape(depth, R, 1, 6 * D)
    SH_M, SC_M, G_M, SH_F, SC_F, G_F = range(6)

    groups = [
        dict(x=x_prompt, B=B1, S=S1, row0=B2, stride=0, latent=False),
        dict(x=x_sample, B=B2, S=S2, row0=0, stride=1, latent=True),
    ]
    rope_hd = _rope_tables(S2, hd)
    rope_cr = _rope_tables(S2, CR)
    w_up_b = w_up.astype(BF16)
    w_down_b = w_down.astype(BF16)

    new_state = {}
    for grp in groups:
        B, S, latent = grp["B"], grp["S"], grp["latent"]
        M = B * S
        row0, stride = grp["row0"], grp["stride"]
        x = grp["x"]
        (h,) = _resid_norm(x, row0, stride, norm=(g_pre_mix[0], mods[0], SC_M, SH_M))
        for i in range(depth):
            j = i // 2
            hf = h.reshape(M, D)
            pdt = BF16 if latent else F32

            def rope(tables, pair, col0, col1):
                return (tables, pair, col0 // LANES, col1 // LANES, S) if latent else None

            if i % 2 == 0:
                wi = w_in_even[j]
                w_in = jnp.concatenate([wi[:, :qa] * _logit_scale(hd), wi[:, qa:]], axis=1).astype(BF16)
                proj = _matmul(hf, w_in, out_dtype=pdt, rope=rope(rope_hd, hd // 4, 0, qa + kva)).reshape(B, S, -1)
                if latent:
                    ctx = (cache_a_k[:, j].reshape(B, P, kva), cache_a_v[:, j].reshape(B, P, kva))
                else:
                    ctx = None
                    new_state["a_k"] = proj[:, :, qa:qa + kva].reshape(B, 1, S, KV, hd)
                    new_state["a_v"] = proj[:, :, qa + kva:qa + 2 * kva].reshape(B, 1, S, KV, hd)
                a = _gqa(proj, 0, proj, qa // hd, proj, (qa + kva) // hd, a_sink[j], ctx, KV=KV, G=G, hd=hd)
                bmix = _pool(proj, (qa + 2 * kva) // Cg, w_pool[j], pool_scale[j])
                y = _matmul_parts([a.reshape(M, qa), bmix.reshape(M, pool_dim)], w_out_even[j].astype(BF16),
                                  out_dtype=BF16)
            else:
                lam_init = 0.8 - 0.6 * math.exp(-0.3 * i)
                wi = w_in_odd[j]
                c1, c2 = CQ + CKV, CQ + CKV + CR
                w_main = jnp.concatenate([wi[:, :c1], wi[:, c2:c2 + dqk] * _logit_scale(DK), wi[:, c2 + dqk:]],
                                         axis=1).astype(BF16)
                w_kr = jnp.concatenate([wi[:, c1:c2], jnp.zeros((D, LANES - CR), F32)], axis=1).astype(BF16)
                o_dq, o_dk, o_dv = c1, c1 + dqk, c1 + 2 * dqk
                proj = _matmul(hf, w_main, out_dtype=pdt, rope=rope(rope_hd, hd // 4, o_dq, o_dv))
                krp = _matmul(hf, w_kr, out_dtype=pdt, tn_pref=LANES,
                              rope=rope(rope_cr, CR // 4, 0, LANES)).reshape(B, S, LANES)
                (cqn,) = _rms_cols(proj, 0, g_q_norm[j], [BF16])
                ckvn_f, ckvn_b = _rms_cols(proj, CQ // CKV, g_kv_norm[j], [F32, BF16])
                wq = w_uq[j].reshape(CQ, CH, CN + CR)
                wq = jnp.concatenate([wq[:, :, :CN].reshape(CQ, CH * CN), wq[:, :, CN:].reshape(CQ, CH * CR)], axis=1)
                wq = wq * _logit_scale(CN + CR)
                qall = _matmul(cqn, wq.astype(BF16), out_dtype=pdt, tn_pref=1024,
                               rope=rope(rope_cr, CR // 4, CH * CN, CH * (CN + CR))).reshape(B, S, CH * (CN + CR))
                wkv = jnp.concatenate([w_uk[j].reshape(CKV, CH * CN), w_uv[j].reshape(CKV, CH * CV)], axis=1).astype(BF16)
                proj3 = proj.reshape(B, S, -1)
                if latent:
                    ckv_all = jnp.concatenate([ckvn_b.reshape(B, S, CKV), cache_c_ckv[:, j].astype(BF16)], axis=1)
                    kr_all = jnp.concatenate([krp[:, :, :CR], cache_c_krope[:, j].astype(BF16)], axis=1)
                    d_ctx = (cache_d_k[:, j].reshape(B, P, dqk), cache_d_v[:, j].reshape(B, P, DH * DV))
                else:
                    ckv_all = ckvn_b.reshape(B, S, CKV)
                    kr_all = krp[:, :, :CR].astype(BF16)
                    d_ctx = None
                    new_state["c_ckv"] = ckvn_f.reshape(B, 1, S, CKV)
                    new_state["c_krope"] = krp[:, :, :CR].reshape(B, 1, S, CR)
                    new_state["d_k"] = proj3[:, :, o_dk:o_dk + dqk].reshape(B, 1, S, DH, 2, DK)
                    new_state["d_v"] = proj3[:, :, o_dv:o_dv + dqk].reshape(B, 1, S, DH, DV)
                Sk = ckv_all.shape[1]
                knv = _matmul(ckv_all.reshape(B * Sk, CKV), wkv, out_dtype=BF16, tn_pref=2048).reshape(B, Sk, -1)
                zk = jnp.zeros_like(kr_all)
                kr2 = jnp.concatenate([kr_all, zk, zk, kr_all], axis=-1)
                c_out = _mla(qall, 0, qall, CH * CN // LANES, knv, kr2, H=CH, dn=CN, dr=CR)
                d_out = _diff(proj3, o_dq // DV, proj3, o_dk // DV, proj3, o_dv // DV, d_ctx,
                              lambda_q1[j], lambda_k1[j], lambda_q2[j], lambda_k2[j], g_subln[j], lam_init, H=DH, dk=DK)
                y = _matmul_parts([c_out.reshape(M, -1), d_out.reshape(M, -1)], w_out_odd[j].astype(BF16),
                                  out_dtype=BF16)
            x, h = _resid_norm(x, row0, stride, resid=(y, g_post_mix[i], mods[i], G_M),
                               norm=(g_pre_ffn[i], mods[i], SC_F, SH_F))
            hmid = _ffn_up(h.reshape(M, D), S, w_up_b, conv_w, conv_b, i)
            f = _matmul(hmid, w_down_b, layer=i, out_dtype=BF16, tm_pref=512, tn_pref=512)
            if i + 1 < depth:
                x, h = _resid_norm(x, row0, stride, resid=(f, g_post_ffn[i], mods[i], G_F),
                                   norm=(g_pre_mix[i + 1], mods[i + 1], SC_M, SH_M))
            else:
                (x,) = _resid_norm(x, row0, stride, resid=(f, g_post_ffn[i], mods[i], G_F))
        grp["out"] = x

    return (groups[0]["out"], groups[1]["out"], new_state["a_k"], new_state["a_v"], new_state["c_ckv"],
            new_state["c_krope"], new_state["d_k"], new_state["d_v"])
```

```python
import functools
import math

import jax
import jax.numpy as jnp
from jax import lax
from jax.experimental import pallas as pl
from jax.experimental.pallas import tpu as pltpu

GRID_W = 64
Q_BLOCK = 128
WINDOW = 128
ROPE_BASE = 10000.0
EPS = 1e-6
NEG_INF = -1e30
POOL_WINDOWS = (2, 4, 8, 16)

LANES = 128
BF16_ROWS = 16
VMEM_LIMIT = 56 * 1024 * 1024

BF16 = jnp.bfloat16
F32 = jnp.float32


def _tile(dim, pref, align):
    best = None
    t = align
    while t <= min(dim, pref):
        if dim % t == 0:
            best = t
        t += align
    return dim if best is None else best


def _params(*sem):
    return pltpu.CompilerParams(dimension_semantics=sem, vmem_limit_bytes=VMEM_LIMIT)


def _mods_kernel(c_ref, w_ref, b_ref, o_ref):
    c = c_ref[...]
    s = (c * (1.0 / (1.0 + jnp.exp(-c)))).astype(BF16)
    o_ref[...] = jnp.dot(s, w_ref[...].astype(BF16), preferred_element_type=F32) + b_ref[...]


def _mods(cond, w_mod, b_mod):
    depth, D, N = w_mod.shape
    R = cond.shape[0]
    tn = _tile(N, 512, LANES)
    return pl.pallas_call(
        _mods_kernel,
        grid=(depth, N // tn),
        in_specs=[
            pl.BlockSpec((R, D), lambda l, j: (0, 0)),
            pl.BlockSpec((None, D, tn), lambda l, j: (l, 0, j)),
            pl.BlockSpec((None, 1, tn), lambda l, j: (l, 0, j)),
        ],
        out_specs=pl.BlockSpec((None, R, tn), lambda l, j: (l, 0, j)),
        out_shape=jax.ShapeDtypeStruct((depth, R, N), F32),
        compiler_params=_params("parallel", "parallel"),
        name="mods",
    )(cond, w_mod, b_mod.reshape(depth, 1, N))


def _rms(x, g):
    return x * lax.rsqrt(jnp.mean(x * x, axis=-1, keepdims=True) + EPS) * g


def _resid_norm_kernel(*refs, has_resid, has_norm):
    refs = list(refs)
    x = refs.pop(0)[0]
    if has_resid:
        y_ref, gpost_ref, gate_ref = refs[:3]
        refs = refs[3:]
        x = x + gate_ref[...] * _rms(y_ref[0].astype(F32), gpost_ref[...])
    if has_norm:
        gpre_ref, sc_ref, sh_ref = refs[:3]
        refs = refs[3:]
    if has_resid:
        refs.pop(0)[0] = x
    if has_norm:
        h = _rms(x, gpre_ref[...]) * (1.0 + sc_ref[...]) + sh_ref[...]
        refs.pop(0)[0] = h.astype(BF16)


def _resid_norm(x, row0, row_stride, *, resid=None, norm=None):
    B, S, D = x.shape
    ts = _tile(S, 256, 8)
    xspec = pl.BlockSpec((1, ts, D), lambda b, s: (b, s, 0))
    gspec = pl.BlockSpec((1, D), lambda b, s: (0, 0))

    def mspec(col):
        return pl.BlockSpec((None, 1, D), lambda b, s: (row0 + b * row_stride, 0, col))

    args, in_specs, out_shape, out_specs = [x], [xspec], [], []
    if resid is not None:
        y, gpost, mods, gate_col = resid
        args += [y.reshape(B, S, D), gpost.reshape(1, D), mods]
        in_specs += [xspec, gspec, mspec(gate_col)]
        out_shape.append(jax.ShapeDtypeStruct((B, S, D), F32))
        out_specs.append(xspec)
    if norm is not None:
        gpre, mods, sc_col, sh_col = norm
        args += [gpre.reshape(1, D), mods, mods]
        in_specs += [gspec, mspec(sc_col), mspec(sh_col)]
        out_shape.append(jax.ShapeDtypeStruct((B, S, D), BF16))
        out_specs.append(xspec)
    outs = pl.pallas_call(
        functools.partial(_resid_norm_kernel, has_resid=resid is not None, has_norm=norm is not None),
        grid=(B, S // ts),
        in_specs=in_specs,
        out_specs=out_specs,
        out_shape=out_shape,
        compiler_params=_params("parallel", "parallel"),
        name="resid_norm",
    )(*args)
    return outs


def _mm_kernel(x_ref, w_ref, o_ref):
    o_ref[...] = jnp.dot(x_ref[...].astype(BF16), w_ref[...], preferred_element_type=F32).astype(o_ref.dtype)


def _mm_parts_kernel(*refs):
    o_ref = refs[-1]
    acc = None
    for x_ref, w_ref in zip(refs[0:-1:2], refs[1:-1:2]):
        d = jnp.dot(x_ref[...], w_ref[...], preferred_element_type=F32)
        acc = d if acc is None else acc + d
    o_ref[...] = acc.astype(o_ref.dtype)


def _matmul_parts(xs, w, *, out_dtype=F32, tm_pref=1024, tn_pref=512):
    M = xs[0].shape[0]
    N = w.shape[1]
    starts = [sum(x.shape[1] for x in xs[:n]) for n in range(len(xs))]
    if any(k0 % x.shape[1] for k0, x in zip(starts, xs)):
        return _matmul(jnp.concatenate(xs, axis=1), w, out_dtype=out_dtype, tm_pref=tm_pref, tn_pref=tn_pref)
    tm = _tile(M, tm_pref, BF16_ROWS)
    tn = _tile(N, tn_pref, LANES)
    in_specs, args, k0 = [], [], 0
    for x in xs:
        kp = x.shape[1]
        in_specs += [pl.BlockSpec((tm, kp), lambda i, j: (i, 0)),
                     pl.BlockSpec((kp, tn), lambda i, j, kb=k0 // kp: (kb, j))]
        args += [x, w]
        k0 += kp
    return pl.pallas_call(
        _mm_parts_kernel,
        grid=(M // tm, N // tn),
        in_specs=in_specs,
        out_specs=pl.BlockSpec((tm, tn), lambda i, j: (i, j)),
        out_shape=jax.ShapeDtypeStruct((M, N), out_dtype),
        compiler_params=_params("parallel", "parallel"),
        name="matmul_parts",
    )(*args)


def _rotate_slab(x, cos, sin, first, pair):
    partner = jnp.where(first, pltpu.roll(x, LANES - pair, 1), pltpu.roll(x, pair, 1))
    return x * cos + partner * sin


def _mm_rope_kernel(x_ref, w_ref, cos_ref, sin_ref, o_ref, *, pair, g0, g1):
    tm, tn = o_ref.shape
    spt = tn // LANES
    acc = jnp.dot(x_ref[...].astype(BF16), w_ref[...], preferred_element_type=F32)
    j = pl.program_id(1)
    lane = lax.broadcasted_iota(jnp.int32, (tm, LANES), 1)
    first = (lane % (2 * pair)) < pair

    def emit(rotated):
        for c in range(spt):
            x = acc[:, c * LANES:(c + 1) * LANES]
            if rotated[c]:
                x = _rotate_slab(x, cos_ref[...], sin_ref[...], first, pair)
            o_ref[:, c * LANES:(c + 1) * LANES] = x.astype(o_ref.dtype)

    full_lo, full_hi = -(-g0 // spt), g1 // spt
    mixed = [t for t in {g0 // spt, (g1 - 1) // spt} if not full_lo <= t < full_hi]
    is_full = (j >= full_lo) & (j < full_hi)
    is_mixed = functools.reduce(jnp.logical_or, [j == t for t in mixed], jnp.bool_(False))
    pl.when(is_full)(lambda: emit([True] * spt))
    for t in mixed:
        pl.when(j == t)(lambda t=t: emit([g0 <= t * spt + c < g1 for c in range(spt)]))
    pl.when(jnp.logical_not(is_full | is_mixed))(lambda: emit([False] * spt))


def _matmul(x, w, *, layer=None, out_dtype=F32, tm_pref=1024, tn_pref=512, rope=None):
    M, K = x.shape
    N = w.shape[-1]
    tn = _tile(N, tn_pref, LANES)
    if layer is None:
        wspec = pl.BlockSpec((K, tn), lambda i, j: (0, j))
    else:
        wspec = pl.BlockSpec((None, K, tn), lambda i, j: (layer, 0, j))
    if rope is None:
        tm = _tile(M, tm_pref, BF16_ROWS)
        kern, extra_specs, extra_args = _mm_kernel, [], []
    else:
        (cos, sin), pair, g0, g1, S = rope
        tm = _tile(S, tm_pref, BF16_ROWS)
        tps = S // tm
        tspec = pl.BlockSpec((tm, LANES), lambda i, j: (i % tps, 0))
        kern = functools.partial(_mm_rope_kernel, pair=pair, g0=g0, g1=g1)
        extra_specs, extra_args = [tspec, tspec], [cos, sin]
    return pl.pallas_call(
        kern,
        grid=(M // tm, N // tn),
        in_specs=[
            pl.BlockSpec((tm, K), lambda i, j: (i, 0)),
            wspec,
        ] + extra_specs,
        out_specs=pl.BlockSpec((tm, tn), lambda i, j: (i, j)),
        out_shape=jax.ShapeDtypeStruct((M, N), out_dtype),
        compiler_params=_params("parallel", "parallel"),
        name="matmul",
    )(x, w, *extra_args)


def _rms_cols_kernel(x_ref, g_ref, *o_refs):
    y = _rms(x_ref[...].astype(F32), g_ref[...])
    for o_ref in o_refs:
        o_ref[...] = y.astype(o_ref.dtype)


def _rms_cols(x, col_block, g, out_dtypes):
    M = x.shape[0]
    W = g.shape[-1]
    tm = _tile(M, 1024, BF16_ROWS)
    spec = pl.BlockSpec((tm, W), lambda i: (i, 0))
    return pl.pallas_call(
        _rms_cols_kernel,
        grid=(M // tm,),
        in_specs=[pl.BlockSpec((tm, W), lambda i: (i, col_block)), pl.BlockSpec((1, W), lambda i: (0, 0))],
        out_specs=[spec] * len(out_dtypes),
        out_shape=[jax.ShapeDtypeStruct((M, W), dt) for dt in out_dtypes],
        compiler_params=_params("parallel"),
        name="rms_cols",
    )(x, g.reshape(1, W))


def _rope_tables(S, width):
    d = width // 2
    half = d // 2
    inv = ROPE_BASE ** (-jnp.arange(half, dtype=F32) / half)
    t = jnp.arange(S)
    row = (t // GRID_W).astype(F32)[:, None] * inv[None, :]
    col = (t % GRID_W).astype(F32)[:, None] * inv[None, :]
    cos = jnp.concatenate([jnp.cos(row), jnp.cos(row), jnp.cos(col), jnp.cos(col)], axis=-1)
    sin = jnp.concatenate([-jnp.sin(row), jnp.sin(row), -jnp.sin(col), jnp.sin(col)], axis=-1)
    reps = LANES // width
    return jnp.tile(cos, (1, reps)), jnp.tile(sin, (1, reps))


LOG2E = math.log2(math.e)
ATTN_ROWS = 256
ATTN_GROUPS = 8


def _heads_per_step(n_heads, S, *col_block_offsets):
    if S > ATTN_ROWS:
        return 1
    return max(d for d in range(1, n_heads + 1)
               if n_heads % d == 0 and all(off % d == 0 for off in col_block_offsets))


def _logit_scale(d):
    return d ** -0.5 * LOG2E


def _dot_nt(a, b):
    return lax.dot_general(a, b, (((1,), (1,)), ((), ())), preferred_element_type=F32)


def _gqa_kernel(*refs, G, hd, tq, nb, S, windowed):
    if windowed:
        q_ref, k_ref, v_ref, sink_ref, ck_ref, cv_ref, o_ref = refs
    else:
        q_ref, k_ref, v_ref, sink_ref, o_ref = refs
    rows = G * tq
    sink = jnp.concatenate([jnp.broadcast_to(sink_ref[0, g:g + 1, :], (tq, LANES)) for g in range(G)], axis=0) * LOG2E
    for blk in range(nb):
        r0 = blk * tq
        q = jnp.concatenate([q_ref[0, r0:r0 + tq, g * hd:(g + 1) * hd] for g in range(G)], axis=0).astype(BF16)
        if windowed:
            n = pl.program_id(2) * nb + blk
            span = 3 * Q_BLOCK
            start = pl.multiple_of(jnp.clip(n * tq - Q_BLOCK, 0, S - span), Q_BLOCK)
            kw = k_ref[0, pl.ds(start, span), :].astype(BF16)
            vw = v_ref[0, pl.ds(start, span), :].astype(BF16)
            qpos = n * tq + lax.broadcasted_iota(jnp.int32, (rows, span), 0) % tq
            kpos = start + lax.broadcasted_iota(jnp.int32, (rows, span), 1)
            s = jnp.where(jnp.abs(kpos - qpos) <= WINDOW, _dot_nt(q, kw), NEG_INF)
            sc = _dot_nt(q, ck_ref[0].astype(BF16))
            m = jnp.maximum(jnp.max(s, axis=-1, keepdims=True), jnp.max(sc, axis=-1, keepdims=True))
        else:
            kw = k_ref[0].astype(BF16)
            vw = v_ref[0].astype(BF16)
            s = _dot_nt(q, kw)
            m = jnp.max(s, axis=-1, keepdims=True)
        p = jnp.exp2(s - m)
        l = jnp.sum(p, axis=-1, keepdims=True)
        acc = jnp.dot(p.astype(BF16), vw, preferred_element_type=F32)
        if windowed:
            pc = jnp.exp2(sc - m)
            l = l + jnp.sum(pc, axis=-1, keepdims=True)
            acc = acc + jnp.dot(pc.astype(BF16), cv_ref[0].astype(BF16), preferred_element_type=F32)
        out = acc / (l + jnp.exp2(sink - m))
        for g in range(G):
            o_ref[0, r0:r0 + tq, g * hd:(g + 1) * hd] = out[g * tq:(g + 1) * tq].astype(o_ref.dtype)


def _gqa(q, q_cb0, k, k_cb0, v, v_cb0, sink, ctx, *, KV, G, hd):
    B, S, _ = q.shape
    windowed = ctx is not None
    tq = Q_BLOCK if windowed else _tile(S, 256, BF16_ROWS)
    nb = max(n for n in (1, 2, 4, 8) if S % (n * tq) == 0) if windowed else 1
    ts = nb * tq
    sink3 = jnp.broadcast_to(sink.astype(F32).reshape(KV, G, 1), (KV, G, LANES))
    in_specs = [
        pl.BlockSpec((1, ts, G * hd), lambda b, h, n: (b, n, q_cb0 + h)),
        pl.BlockSpec((1, S, hd), lambda b, h, n: (b, 0, k_cb0 + h)),
        pl.BlockSpec((1, S, hd), lambda b, h, n: (b, 0, v_cb0 + h)),
        pl.BlockSpec((1, G, LANES), lambda b, h, n: (h, 0, 0)),
    ]
    args = [q, k, v, sink3]
    if windowed:
        ck, cv = ctx
        P = ck.shape[1]
        cspec = pl.BlockSpec((1, P, hd), lambda b, h, n: (b, 0, h))
        in_specs += [cspec, cspec]
        args += [ck, cv]
    return pl.pallas_call(
        functools.partial(_gqa_kernel, G=G, hd=hd, tq=tq, nb=nb, S=S, windowed=windowed),
        grid=(B, KV, S // ts),
        in_specs=in_specs,
        out_specs=pl.BlockSpec((1, ts, G * hd), lambda b, h, n: (b, n, h)),
        out_shape=jax.ShapeDtypeStruct((B, S, KV * G * hd), BF16),
        compiler_params=_params("parallel", "parallel", "parallel"),
        name="gqa_attention",
    )(*args)


POOL_PAD = 16


def _pool_kernel(u_ref, w_ref, ps_ref, o_ref, pad_ref, *, S):
    grp = pl.program_id(1)
    u = u_ref[0].astype(F32)
    C = u.shape[-1]
    pad_ref[0:POOL_PAD, :] = jnp.zeros((POOL_PAD, C), F32)
    pad_ref[POOL_PAD + S:2 * POOL_PAD + S, :] = jnp.zeros((POOL_PAD, C), F32)
    pad_ref[POOL_PAD:POOL_PAD + S, :] = u
    t = lax.broadcasted_iota(jnp.int32, (S, 1), 0)
    for gi, win in enumerate(POOL_WINDOWS):
        @pl.when(grp == gi)
        def _(win=win):
            half = win // 2
            total = pad_ref[POOL_PAD - half:POOL_PAD - half + S, :]
            for off in range(-half + 1, half):
                total = total + pad_ref[POOL_PAD + off:POOL_PAD + off + S, :]
            count = (jnp.clip(t + half, 0, S) - jnp.clip(t - half, 0, S)).astype(F32)
            pooled = (total / count - u).astype(BF16)
            y = jnp.dot(pooled, w_ref[0].astype(BF16), preferred_element_type=F32) * ps_ref[...]
            o_ref[0] = y.astype(o_ref.dtype)


def _pool(u, u_cb0, w_pool, pool_scale):
    B, S, _ = u.shape
    NG, Cg, _ = w_pool.shape
    assert NG == len(POOL_WINDOWS)
    return pl.pallas_call(
        functools.partial(_pool_kernel, S=S),
        grid=(B, NG),
        in_specs=[
            pl.BlockSpec((1, S, Cg), lambda b, g: (b, 0, u_cb0 + g)),
            pl.BlockSpec((1, Cg, Cg), lambda b, g: (g, 0, 0)),
            pl.BlockSpec((1, Cg), lambda b, g: (0, g)),
        ],
        out_specs=pl.BlockSpec((1, S, Cg), lambda b, g: (b, 0, g)),
        out_shape=jax.ShapeDtypeStruct((B, S, NG * Cg), BF16),
        scratch_shapes=[pltpu.VMEM((S + 2 * POOL_PAD, Cg), F32)],
        compiler_params=_params("parallel", "parallel"),
        name="pool_mixer",
    )(u, w_pool, pool_scale.reshape(1, NG * Cg))


def _mla_kernel(qn_ref, qr_ref, kn_ref, kr_ref, v_ref, o_ref, *, dn, hpb):
    tq = qn_ref.shape[1]
    ones = jnp.ones((kn_ref.shape[1], LANES), BF16)
    for r0 in range(0, tq, ATTN_ROWS):
        rows = slice(r0, min(r0 + ATTN_ROWS, tq))
        for hp in range(hpb):
            qr = qr_ref[0, rows, hp * LANES:(hp + 1) * LANES].astype(BF16)
            for e in range(2):
                cols = slice((2 * hp + e) * dn, (2 * hp + e + 1) * dn)
                q = jnp.concatenate([qn_ref[0, rows, cols].astype(BF16), qr], axis=-1)
                k = jnp.concatenate([kn_ref[0, :, cols], kr_ref[0, :, e * LANES:(e + 1) * LANES]], axis=-1)
                s = _dot_nt(q, k)
                p = jnp.exp2(s - jnp.max(s, axis=-1, keepdims=True)).astype(BF16)
                acc = jnp.dot(p, jnp.concatenate([v_ref[0, :, cols], ones], axis=-1), preferred_element_type=F32)
                o_ref[0, rows, cols] = (acc[:, 0:dn] / acc[:, dn:2 * dn]).astype(o_ref.dtype)


def _mla(qn, qn_cb0, qr, qr_cb0, knv, kr2, *, H, dn, dr):
    B, S, _ = qn.shape
    Sk = knv.shape[1]
    assert dn == LANES and 2 * dr == LANES
    tq = _tile(S, ATTN_GROUPS * ATTN_ROWS, BF16_ROWS)
    HP = H // 2
    hpb = _heads_per_step(HP, S, qn_cb0, qr_cb0)
    nh = HP // hpb
    return pl.pallas_call(
        functools.partial(_mla_kernel, dn=dn, hpb=hpb),
        grid=(B, nh, S // tq),
        in_specs=[
            pl.BlockSpec((1, tq, 2 * dn * hpb), lambda b, h, n: (b, n, qn_cb0 // hpb + h)),
            pl.BlockSpec((1, tq, LANES * hpb), lambda b, h, n: (b, n, qr_cb0 // hpb + h)),
            pl.BlockSpec((1, Sk, 2 * dn * hpb), lambda b, h, n: (b, 0, h)),
            pl.BlockSpec((1, Sk, 2 * LANES), lambda b, h, n: (b, 0, 0)),
            pl.BlockSpec((1, Sk, 2 * dn * hpb), lambda b, h, n: (b, 0, nh + h)),
        ],
        out_specs=pl.BlockSpec((1, tq, 2 * dn * hpb), lambda b, h, n: (b, n, h)),
        out_shape=jax.ShapeDtypeStruct((B, S, H * dn), BF16),
        compiler_params=_params("parallel", "parallel", "parallel"),
        name="mla_attention",
    )(qn, qr, knv, kr2, knv)


def _diff_kernel(*refs, dk, lam_init, has_ctx, hb):
    if has_ctx:
        q_ref, k_ref, v_ref, ck_ref, cv_ref, l1_ref, l2_ref, gs_ref, o_ref = refs
        key_refs, val_refs = (k_ref, ck_ref), (v_ref, cv_ref)
    else:
        q_ref, k_ref, v_ref, l1_ref, l2_ref, gs_ref, o_ref = refs
        key_refs, val_refs = (k_ref,), (v_ref,)
    lam = (jnp.exp(jnp.sum(l1_ref[0:1, :] * l1_ref[1:2, :], axis=-1, keepdims=True))
           - jnp.exp(jnp.sum(l2_ref[0:1, :] * l2_ref[1:2, :], axis=-1, keepdims=True)) + lam_init)
    tq = q_ref.shape[1]
    dv = 2 * dk
    for r0 in range(0, tq, ATTN_ROWS):
        rows = slice(r0, min(r0 + ATTN_ROWS, tq))
        for hh in range(hb):
            out = None
            for mi in range(2):
                cols = slice(hh * dv + mi * dk, hh * dv + (mi + 1) * dk)
                q = q_ref[0, rows, cols].astype(BF16)
                ss = [_dot_nt(q, kr[0, :, cols].astype(BF16)) for kr in key_refs]
                m = functools.reduce(jnp.maximum, [jnp.max(s, axis=-1, keepdims=True) for s in ss])
                es = [jnp.exp2(s - m) for s in ss]
                tot = functools.reduce(jnp.add, [jnp.sum(e, axis=-1, keepdims=True) for e in es])
                pv = functools.reduce(jnp.add, [
                    jnp.dot(e.astype(BF16), vr[0, :, hh * dv:(hh + 1) * dv].astype(BF16), preferred_element_type=F32)
                    for e, vr in zip(es, val_refs)])
                out = pv / tot if mi == 0 else out - pv * (lam / tot)
            o_ref[0, rows, hh * dv:(hh + 1) * dv] = (_rms(out, gs_ref[...]) * (1.0 - lam_init)).astype(o_ref.dtype)


def _diff(q, q_cb0, k, k_cb0, v, v_cb0, ctx, lam_q1, lam_k1, lam_q2, lam_k2, g_subln, lam_init, *, H, dk):
    B, S, _ = q.shape
    dv = 2 * dk
    tq = _tile(S, ATTN_GROUPS * ATTN_ROWS, BF16_ROWS)
    l1 = jnp.stack([lam_q1, lam_k1]).astype(F32)
    l2 = jnp.stack([lam_q2, lam_k2]).astype(F32)
    lspec = pl.BlockSpec((2, dk), lambda b, h, n: (0, 0))
    hb = 1 if ctx is not None else _heads_per_step(H, S, q_cb0, k_cb0, v_cb0)
    in_specs = [
        pl.BlockSpec((1, tq, dv * hb), lambda b, h, n: (b, n, q_cb0 // hb + h)),
        pl.BlockSpec((1, S, dv * hb), lambda b, h, n: (b, 0, k_cb0 // hb + h)),
        pl.BlockSpec((1, S, dv * hb), lambda b, h, n: (b, 0, v_cb0 // hb + h)),
    ]
    args = [q, k, v]
    if ctx is not None:
        P = ctx[0].shape[1]
        cspec = pl.BlockSpec((1, P, dv), lambda b, h, n: (b, 0, h))
        in_specs += [cspec, cspec]
        args += list(ctx)
    in_specs += [lspec, lspec, pl.BlockSpec((1, dv), lambda b, h, n: (0, 0))]
    args += [l1, l2, g_subln.reshape(1, dv)]
    return pl.pallas_call(
        functools.partial(_diff_kernel, dk=dk, lam_init=lam_init, has_ctx=ctx is not None, hb=hb),
        grid=(B, H // hb, S // tq),
        in_specs=in_specs,
        out_specs=pl.BlockSpec((1, tq, dv * hb), lambda b, h, n: (b, n, h)),
        out_shape=jax.ShapeDtypeStruct((B, S, H * dv), BF16),
        compiler_params=_params("parallel", "parallel", "parallel"),
        name="diff_attention",
    )(*args)


Z_PAD = 8
FFN_ROW_CHUNK = 32
FFN_DOT_ROWS = 1024


def _ffn_up_kernel(h_ref, wg_ref, wv_ref, cwg_ref, cwv_ref, cbg_ref, cbv_ref, o_ref, z_ref, *, tm, tn, S, rc, rm):
    zero = jnp.zeros((Z_PAD, 2 * tn), F32)
    z_ref[0:Z_PAD, :] = zero
    z_ref[Z_PAD + tm:2 * Z_PAD + tm, :] = zero
    for r0 in range(0, tm, rm):
        h = h_ref[r0:r0 + rm, :]
        z_ref[Z_PAD + r0:Z_PAD + r0 + rm, 0:tn] = jnp.dot(h, wg_ref[...], preferred_element_type=F32)
        z_ref[Z_PAD + r0:Z_PAD + r0 + rm, tn:2 * tn] = jnp.dot(h, wv_ref[...], preferred_element_type=F32)

    cw = jnp.concatenate([cwg_ref[...], cwv_ref[...]], axis=1)
    cb = jnp.concatenate([cbg_ref[...], cbv_ref[...]], axis=1)
    span = rc + 2 * Z_PAD
    for r0 in range(0, tm, rc):
        zf = z_ref[r0:r0 + span, :]
        zc = zf[Z_PAD:Z_PAD + rc]
        zp = pltpu.roll(zf, 1, 0)[Z_PAD:Z_PAD + rc]
        zn = pltpu.roll(zf, span - 1, 0)[Z_PAD:Z_PAD + rc]
        if tm > S:
            pos = (r0 + lax.broadcasted_iota(jnp.int32, (rc, 1), 0)) % S
            zp = jnp.where(pos != 0, zp, 0.0)
            zn = jnp.where(pos != S - 1, zn, 0.0)
        y = zp * cw[0:1, :] + zc * cw[1:2, :] + zn * cw[2:3, :] + cb
        g = y[:, 0:tn]
        o_ref[r0:r0 + rc, :] = (g * (1.0 / (1.0 + jnp.exp(-g))) * y[:, tn:2 * tn]).astype(o_ref.dtype)


def _ffn_up(h, S, w_up, conv_w, conv_b, layer):
    M, D = h.shape
    L = w_up.shape[0]
    F = w_up.shape[2] // 2
    tm = _tile(M, max(2048, S), S)
    assert tm % S == 0
    tn = _tile(F, 256, LANES)
    nj = F // tn
    rc = _tile(tm, FFN_ROW_CHUNK, 8)
    rm = _tile(tm, FFN_DOT_ROWS, BF16_ROWS)
    cb = conv_b.reshape(L, 1, 2 * F)
    gate_tile = lambda i, j: (layer, 0, j)
    val_tile = lambda i, j: (layer, 0, nj + j)
    return pl.pallas_call(
        functools.partial(_ffn_up_kernel, tm=tm, tn=tn, S=S, rc=rc, rm=rm),
        grid=(M // tm, nj),
        in_specs=[
            pl.BlockSpec((tm, D), lambda i, j: (i, 0)),
            pl.BlockSpec((None, D, tn), gate_tile),
            pl.BlockSpec((None, D, tn), val_tile),
            pl.BlockSpec((None, 3, tn), gate_tile),
            pl.BlockSpec((None, 3, tn), val_tile),
            pl.BlockSpec((None, 1, tn), gate_tile),
            pl.BlockSpec((None, 1, tn), val_tile),
        ],
        out_specs=pl.BlockSpec((tm, tn), lambda i, j: (i, j)),
        out_shape=jax.ShapeDtypeStruct((M, F), BF16),
        scratch_shapes=[pltpu.VMEM((tm + 2 * Z_PAD, 2 * tn), F32)],
        compiler_params=_params("parallel", "parallel"),
        name="ffn_up_conv_gate",
    )(h, w_up, w_up, conv_w, conv_w, cb, cb)


def kernel(x_prompt, x_sample, cache_a_k, cache_a_v, cache_c_ckv, cache_c_krope, cache_d_k, cache_d_v, c, c_ctx, w_mod, b_mod, g_pre_mix, g_post_mix, g_pre_ffn, g_post_ffn, w_in_even, w_out_even, a_sink, w_pool, pool_scale, w_in_odd, w_out_odd, g_q_norm, w_uq, g_kv_norm, w_uk, w_uv, lambda_q1, lambda_k1, lambda_q2, lambda_k2, g_subln, w_up, conv_w, conv_b, w_down):
    B1, S1, D = x_prompt.shape
    B2, S2, _ = x_sample.shape
    depth = w_mod.shape[0]
    P = cache_a_k.shape[2]
    KV, hd = cache_a_k.shape[3], cache_a_k.shape[4]
    AH = a_sink.shape[1]
    G = AH // KV
    NG, Cg = w_pool.shape[1], w_pool.shape[2]
    qa, kva, pool_dim = AH * hd, KV * hd, NG * Cg
    CQ = g_q_norm.shape[1]
    CKV, CH, CN = w_uk.shape[1], w_uk.shape[2], w_uk.shape[3]
    CR = cache_c_krope.shape[3]
    CV = w_uv.shape[3]
    DH, DK = cache_d_k.shape[3], cache_d_k.shape[5]
    DV = 2 * DK
    dqk = DH * 2 * DK
    assert hd == LANES and CN == LANES and CV == CN and 2 * CR == LANES and DK == LANES

    R = -(-(B2 + 1) // 8) * 8
    cond = jnp.zeros((R, D), F32).at[:B2].set(c).at[B2].set(c_ctx)
    mods = _mods(cond, w_mod, b_mod).reshape(depth, R, 1, 6 * D)
    SH_M, SC_M, G_M, SH_F, SC_F, G_F = range(6)

    groups = [
        dict(x=x_prompt, B=B1, S=S1, row0=B2, stride=0, latent=False),
        dict(x=x_sample, B=B2, S=S2, row0=0, stride=1, latent=True),
    ]
    rope_hd = _rope_tables(S2, hd)
    rope_cr = _rope_tables(S2, CR)
    w_up_b = w_up.astype(BF16)
    w_down_b = w_down.astype(BF16)

    new_state = {}
    for grp in groups:
        B, S, latent = grp["B"], grp["S"], grp["latent"]
        M = B * S
        row0, stride = grp["row0"], grp["stride"]
        x = grp["x"]
        (h,) = _resid_norm(x, row0, stride, norm=(g_pre_mix[0], mods[0], SC_M, SH_M))
        for i in range(depth):
            j = i // 2
            hf = h.reshape(M, D)
            pdt = BF16 if latent else F32

            def rope(tables, pair, col0, col1):
                return (tables, pair, col0 // LANES, col1 // LANES, S) if latent else None

            if i % 2 == 0:
                wi = w_in_even[j]
                w_in = jnp.concatenate([wi[:, :qa] * _logit_scale(hd), wi[:, qa:]], axis=1).astype(BF16)
                proj = _matmul(hf, w_in, out_dtype=pdt, rope=rope(rope_hd, hd // 4, 0, qa + kva)).reshape(B, S, -1)
                if latent:
                    ctx = (cache_a_k[:, j].reshape(B, P, kva), cache_a_v[:, j].reshape(B, P, kva))
                else:
                    ctx = None
                    new_state["a_k"] = proj[:, :, qa:qa + kva].reshape(B, 1, S, KV, hd)
                    new_state["a_v"] = proj[:, :, qa + kva:qa + 2 * kva].reshape(B, 1, S, KV, hd)
                a = _gqa(proj, 0, proj, qa // hd, proj, (qa + kva) // hd, a_sink[j], ctx, KV=KV, G=G, hd=hd)
                bmix = _pool(proj, (qa + 2 * kva) // Cg, w_pool[j], pool_scale[j])
                y = _matmul_parts([a.reshape(M, qa), bmix.reshape(M, pool_dim)], w_out_even[j].astype(BF16),
                                  out_dtype=BF16)
            else:
                lam_init = 0.8 - 0.6 * math.exp(-0.3 * i)
                wi = w_in_odd[j]
                c1, c2 = CQ + CKV, CQ + CKV + CR
                w_main = jnp.concatenate([wi[:, :c1], wi[:, c2:c2 + dqk] * _logit_scale(DK), wi[:, c2 + dqk:]],
                                         axis=1).astype(BF16)
                w_kr = jnp.concatenate([wi[:, c1:c2], jnp.zeros((D, LANES - CR), F32)], axis=1).astype(BF16)
                o_dq, o_dk, o_dv = c1, c1 + dqk, c1 + 2 * dqk
                proj = _matmul(hf, w_main, out_dtype=pdt, rope=rope(rope_hd, hd // 4, o_dq, o_dv))
                krp = _matmul(hf, w_kr, out_dtype=pdt, tn_pref=LANES,
                              rope=rope(rope_cr, CR // 4, 0, LANES)).reshape(B, S, LANES)
                (cqn,) = _rms_cols(proj, 0, g_q_norm[j], [BF16])
                ckvn_f, ckvn_b = _rms_cols(proj, CQ // CKV, g_kv_norm[j], [F32, BF16])
                wq = w_uq[j].reshape(CQ, CH, CN + CR)
                wq = jnp.concatenate([wq[:, :, :CN].reshape(CQ, CH * CN), wq[:, :, CN:].reshape(CQ, CH * CR)], axis=1)
                wq = wq * _logit_scale(CN + CR)
                qall = _matmul(cqn, wq.astype(BF16), out_dtype=pdt, tn_pref=1024,
                               rope=rope(rope_cr, CR // 4, CH * CN, CH * (CN + CR))).reshape(B, S, CH * (CN + CR))
                wkv = jnp.concatenate([w_uk[j].reshape(CKV, CH * CN), w_uv[j].reshape(CKV, CH * CV)], axis=1).astype(BF16)
                proj3 = proj.reshape(B, S, -1)
                if latent:
                    ckv_all = jnp.concatenate([ckvn_b.reshape(B, S, CKV), cache_c_ckv[:, j].astype(BF16)], axis=1)
                    kr_all = jnp.concatenate([krp[:, :, :CR], cache_c_krope[:, j].astype(BF16)], axis=1)
                    d_ctx = (cache_d_k[:, j].reshape(B, P, dqk), cache_d_v[:, j].reshape(B, P, DH * DV))
                else:
                    ckv_all = ckvn_b.reshape(B, S, CKV)
                    kr_all = krp[:, :, :CR].astype(BF16)
                    d_ctx = None
                    new_state["c_ckv"] = ckvn_f.reshape(B, 1, S, CKV)
                    new_state["c_krope"] = krp[:, :, :CR].reshape(B, 1, S, CR)
                    new_state["d_k"] = proj3[:, :, o_dk:o_dk + dqk].reshape(B, 1, S, DH, 2, DK)
                    new_state["d_v"] = proj3[:, :, o_dv:o_dv + dqk].reshape(B, 1, S, DH, DV)
                Sk = ckv_all.shape[1]
                knv = _matmul(ckv_all.reshape(B * Sk, CKV), wkv, out_dtype=BF16, tn_pref=2048).reshape(B, Sk, -1)
                zk = jnp.zeros_like(kr_all)
                kr2 = jnp.concatenate([kr_all, zk, zk, kr_all], axis=-1)
                c_out = _mla(qall, 0, qall, CH * CN // LANES, knv, kr2, H=CH, dn=CN, dr=CR)
                d_out = _diff(proj3, o_dq // DV, proj3, o_dk // DV, proj3, o_dv // DV, d_ctx,
                              lambda_q1[j], lambda_k1[j], lambda_q2[j], lambda_k2[j], g_subln[j], lam_init, H=DH, dk=DK)
                y = _matmul_parts([c_out.reshape(M, -1), d_out.reshape(M, -1)], w_out_odd[j].astype(BF16),
                                  out_dtype=BF16)
            x, h = _resid_norm(x, row0, stride, resid=(y, g_post_mix[i], mods[i], G_M),
                               norm=(g_pre_ffn[i], mods[i], SC_F, SH_F))
            hmid = _ffn_up(h.reshape(M, D), S, w_up_b, conv_w, conv_b, i)
            f = _matmul(hmid, w_down_b, layer=i, out_dtype=BF16, tm_pref=512, tn_pref=512)
            if i + 1 < depth:
                x, h = _resid_norm(x, row0, stride, resid=(f, g_post_ffn[i], mods[i], G_F),
                                   norm=(g_pre_mix[i + 1], mods[i + 1], SC_M, SH_M))
            else:
                (x,) = _resid_norm(x, row0, stride, resid=(f, g_post_ffn[i], mods[i], G_F))
        grp["out"] = x

    return (groups[0]["out"], groups[1]["out"], new_state["a_k"], new_state["a_v"], new_state["c_ckv"],
            new_state["c_krope"], new_state["d_k"], new_state["d_v"])
```

```python
import functools
import math

import jax
import jax.numpy as jnp
from jax import lax
from jax.experimental import pallas as pl
from jax.experimental.pallas import tpu as pltpu

GRID_W = 64
Q_BLOCK = 128
WINDOW = 128
ROPE_BASE = 10000.0
EPS = 1e-6
NEG_INF = -1e30
POOL_WINDOWS = (2, 4, 8, 16)

LANES = 128
BF16_ROWS = 16
VMEM_LIMIT = 58 * 1024 * 1024
WIDE_TILE = 1408

BF16 = jnp.bfloat16
F32 = jnp.float32


def _tile(dim, pref, align):
    best = None
    t = align
    while t <= min(dim, pref):
        if dim % t == 0:
            best = t
        t += align
    return dim if best is None else best


def _params(*sem):
    return pltpu.CompilerParams(dimension_semantics=sem, vmem_limit_bytes=VMEM_LIMIT)


def _mods_kernel(c_ref, w_ref, b_ref, o_ref):
    c = c_ref[...]
    s = (c * (1.0 / (1.0 + jnp.exp(-c)))).astype(BF16)
    o_ref[...] = jnp.dot(s, w_ref[...].astype(BF16), preferred_element_type=F32) + b_ref[...]


def _mods(cond, w_mod, b_mod):
    depth, D, N = w_mod.shape
    R = cond.shape[0]
    tn = _tile(N, 512, LANES)
    return pl.pallas_call(
        _mods_kernel,
        grid=(depth, N // tn),
        in_specs=[
            pl.BlockSpec((R, D), lambda l, j: (0, 0)),
            pl.BlockSpec((None, D, tn), lambda l, j: (l, 0, j)),
            pl.BlockSpec((None, 1, tn), lambda l, j: (l, 0, j)),
        ],
        out_specs=pl.BlockSpec((None, R, tn), lambda l, j: (l, 0, j)),
        out_shape=jax.ShapeDtypeStruct((depth, R, N), F32),
        compiler_params=_params("parallel", "parallel"),
        name="mods",
    )(cond, w_mod, b_mod.reshape(depth, 1, N))


def _rms(x, g):
    return x * lax.rsqrt(jnp.mean(x * x, axis=-1, keepdims=True) + EPS) * g


def _resid_norm_kernel(*refs, has_resid, has_norm):
    refs = list(refs)
    x = refs.pop(0)[0]
    if has_resid:
        y_ref, gpost_ref, gate_ref = refs[:3]
        refs = refs[3:]
        x = x + gate_ref[...] * _rms(y_ref[0].astype(F32), gpost_ref[...])
    if has_norm:
        gpre_ref, sc_ref, sh_ref = refs[:3]
        refs = refs[3:]
    if has_resid:
        refs.pop(0)[0] = x
    if has_norm:
        h = _rms(x, gpre_ref[...]) * (1.0 + sc_ref[...]) + sh_ref[...]
        refs.pop(0)[0] = h.astype(BF16)


def _resid_norm(x, row0, row_stride, *, resid=None, norm=None):
    B, S, D = x.shape
    ts = _tile(S, 256, 8)
    xspec = pl.BlockSpec((1, ts, D), lambda b, s: (b, s, 0))
    gspec = pl.BlockSpec((1, D), lambda b, s: (0, 0))

    def mspec(col):
        return pl.BlockSpec((None, 1, D), lambda b, s: (row0 + b * row_stride, 0, col))

    args, in_specs, out_shape, out_specs = [x], [xspec], [], []
    if resid is not None:
        y, gpost, mods, gate_col = resid
        args += [y.reshape(B, S, D), gpost.reshape(1, D), mods]
        in_specs += [xspec, gspec, mspec(gate_col)]
        out_shape.append(jax.ShapeDtypeStruct((B, S, D), F32))
        out_specs.append(xspec)
    if norm is not None:
        gpre, mods, sc_col, sh_col = norm
        args += [gpre.reshape(1, D), mods, mods]
        in_specs += [gspec, mspec(sc_col), mspec(sh_col)]
        out_shape.append(jax.ShapeDtypeStruct((B, S, D), BF16))
        out_specs.append(xspec)
    outs = pl.pallas_call(
        functools.partial(_resid_norm_kernel, has_resid=resid is not None, has_norm=norm is not None),
        grid=(B, S // ts),
        in_specs=in_specs,
        out_specs=out_specs,
        out_shape=out_shape,
        compiler_params=_params("parallel", "parallel"),
        name="resid_norm",
    )(*args)
    return outs


def _mm_kernel(x_ref, w_ref, o_ref):
    o_ref[...] = jnp.dot(x_ref[...].astype(BF16), w_ref[...], preferred_element_type=F32).astype(o_ref.dtype)


def _mm_parts_kernel(*refs):
    o_ref = refs[-1]
    acc = None
    for x_ref, w_ref in zip(refs[0:-1:2], refs[1:-1:2]):
        d = jnp.dot(x_ref[...], w_ref[...], preferred_element_type=F32)
        acc = d if acc is None else acc + d
    o_ref[...] = acc.astype(o_ref.dtype)


def _matmul_parts(xs, w, *, out_dtype=F32, tm_pref=1024, tn_pref=512):
    M = xs[0].shape[0]
    N = w.shape[1]
    starts = [sum(x.shape[1] for x in xs[:n]) for n in range(len(xs))]
    if any(k0 % x.shape[1] for k0, x in zip(starts, xs)):
        return _matmul(jnp.concatenate(xs, axis=1), w, out_dtype=out_dtype, tm_pref=tm_pref, tn_pref=tn_pref)
    tm = _tile(M, tm_pref, BF16_ROWS)
    tn = _tile(N, tn_pref, LANES)
    in_specs, args, k0 = [], [], 0
    for x in xs:
        kp = x.shape[1]
        in_specs += [pl.BlockSpec((tm, kp), lambda i, j: (i, 0)),
                     pl.BlockSpec((kp, tn), lambda i, j, kb=k0 // kp: (kb, j))]
        args += [x, w]
        k0 += kp
    return pl.pallas_call(
        _mm_parts_kernel,
        grid=(M // tm, N // tn),
        in_specs=in_specs,
        out_specs=pl.BlockSpec((tm, tn), lambda i, j: (i, j)),
        out_shape=jax.ShapeDtypeStruct((M, N), out_dtype),
        compiler_params=_params("parallel", "parallel"),
        name="matmul_parts",
    )(*args)


def _rotate_slab(x, cos, sin, first, pair):
    partner = jnp.where(first, pltpu.roll(x, LANES - pair, 1), pltpu.roll(x, pair, 1))
    return x * cos + partner * sin


def _mm_rope_kernel(x_ref, w_ref, cos_ref, sin_ref, o_ref, *, pair, g0, g1):
    tm, tn = o_ref.shape
    spt = tn // LANES
    acc = jnp.dot(x_ref[...].astype(BF16), w_ref[...], preferred_element_type=F32)
    j = pl.program_id(1)
    lane = lax.broadcasted_iota(jnp.int32, (tm, LANES), 1)
    first = (lane % (2 * pair)) < pair

    def emit(rotated):
        for c in range(spt):
            x = acc[:, c * LANES:(c + 1) * LANES]
            if rotated[c]:
                x = _rotate_slab(x, cos_ref[...], sin_ref[...], first, pair)
            o_ref[:, c * LANES:(c + 1) * LANES] = x.astype(o_ref.dtype)

    full_lo, full_hi = -(-g0 // spt), g1 // spt
    mixed = [t for t in {g0 // spt, (g1 - 1) // spt} if not full_lo <= t < full_hi]
    is_full = (j >= full_lo) & (j < full_hi)
    is_mixed = functools.reduce(jnp.logical_or, [j == t for t in mixed], jnp.bool_(False))
    pl.when(is_full)(lambda: emit([True] * spt))
    for t in mixed:
        pl.when(j == t)(lambda t=t: emit([g0 <= t * spt + c < g1 for c in range(spt)]))
    pl.when(jnp.logical_not(is_full | is_mixed))(lambda: emit([False] * spt))


def _matmul(x, w, *, layer=None, out_dtype=F32, tm_pref=1024, tn_pref=512, rope=None):
    M, K = x.shape
    N = w.shape[-1]
    tn = _tile(N, tn_pref, LANES)
    if layer is None:
        wspec = pl.BlockSpec((K, tn), lambda i, j: (0, j))
    else:
        wspec = pl.BlockSpec((None, K, tn), lambda i, j: (layer, 0, j))
    if rope is None:
        tm = _tile(M, tm_pref, BF16_ROWS)
        kern, extra_specs, extra_args = _mm_kernel, [], []
    else:
        (cos, sin), pair, g0, g1, S = rope
        tm = _tile(S, tm_pref, BF16_ROWS)
        tps = S // tm
        tspec = pl.BlockSpec((tm, LANES), lambda i, j: (i % tps, 0))
        kern = functools.partial(_mm_rope_kernel, pair=pair, g0=g0, g1=g1)
        extra_specs, extra_args = [tspec, tspec], [cos, sin]
    return pl.pallas_call(
        kern,
        grid=(M // tm, N // tn),
        in_specs=[
            pl.BlockSpec((tm, K), lambda i, j: (i, 0)),
            wspec,
        ] + extra_specs,
        out_specs=pl.BlockSpec((tm, tn), lambda i, j: (i, j)),
        out_shape=jax.ShapeDtypeStruct((M, N), out_dtype),
        compiler_params=_params("parallel", "parallel"),
        name="matmul",
    )(x, w, *extra_args)


def _rms_cols_kernel(x_ref, g_ref, *o_refs):
    y = _rms(x_ref[...].astype(F32), g_ref[...])
    for o_ref in o_refs:
        o_ref[...] = y.astype(o_ref.dtype)


def _rms_cols(x, col_block, g, out_dtypes):
    M = x.shape[0]
    W = g.shape[-1]
    tm = _tile(M, 1024, BF16_ROWS)
    spec = pl.BlockSpec((tm, W), lambda i: (i, 0))
    return pl.pallas_call(
        _rms_cols_kernel,
        grid=(M // tm,),
        in_specs=[pl.BlockSpec((tm, W), lambda i: (i, col_block)), pl.BlockSpec((1, W), lambda i: (0, 0))],
        out_specs=[spec] * len(out_dtypes),
        out_shape=[jax.ShapeDtypeStruct((M, W), dt) for dt in out_dtypes],
        compiler_params=_params("parallel"),
        name="rms_cols",
    )(x, g.reshape(1, W))


def _rope_tables(S, width):
    d = width // 2
    half = d // 2
    inv = ROPE_BASE ** (-jnp.arange(half, dtype=F32) / half)
    t = jnp.arange(S)
    row = (t // GRID_W).astype(F32)[:, None] * inv[None, :]
    col = (t % GRID_W).astype(F32)[:, None] * inv[None, :]
    cos = jnp.concatenate([jnp.cos(row), jnp.cos(row), jnp.cos(col), jnp.cos(col)], axis=-1)
    sin = jnp.concatenate([-jnp.sin(row), jnp.sin(row), -jnp.sin(col), jnp.sin(col)], axis=-1)
    reps = LANES // width
    return jnp.tile(cos, (1, reps)), jnp.tile(sin, (1, reps))


LOG2E = math.log2(math.e)
ATTN_ROWS = 256
ATTN_GROUPS = 8


def _heads_per_step(n_heads, S, *col_block_offsets):
    if S > ATTN_ROWS:
        return 1
    return max(d for d in range(1, n_heads + 1)
               if n_heads % d == 0 and all(off % d == 0 for off in col_block_offsets))


def _logit_scale(d):
    return d ** -0.5 * LOG2E


def _dot_nt(a, b):
    return lax.dot_general(a, b, (((1,), (1,)), ((), ())), preferred_element_type=F32)


def _gqa_kernel(*refs, G, hd, tq, nb, S, windowed):
    if windowed:
        q_ref, k_ref, v_ref, sink_ref, ck_ref, cv_ref, o_ref = refs
    else:
        q_ref, k_ref, v_ref, sink_ref, o_ref = refs
    rows = G * tq
    sink = jnp.concatenate([jnp.broadcast_to(sink_ref[0, g:g + 1, :], (tq, LANES)) for g in range(G)], axis=0) * LOG2E
    if windowed:
        span = 3 * Q_BLOCK
        rel0 = (lax.broadcasted_iota(jnp.int32, (rows, span), 1)
                - lax.broadcasted_iota(jnp.int32, (rows, span), 0) % tq)
    for blk in range(nb):
        r0 = blk * tq
        q = jnp.concatenate([q_ref[0, r0:r0 + tq, g * hd:(g + 1) * hd] for g in range(G)], axis=0).astype(BF16)
        if windowed:
            n = pl.program_id(2) * nb + blk
            start = pl.multiple_of(jnp.clip(n * tq - Q_BLOCK, 0, S - span), Q_BLOCK)
            kw = k_ref[0, pl.ds(start, span), :].astype(BF16)
            vw = v_ref[0, pl.ds(start, span), :].astype(BF16)
            s = jnp.where(jnp.abs(rel0 + (start - n * tq)) <= WINDOW, _dot_nt(q, kw), NEG_INF)
            sc = _dot_nt(q, ck_ref[0].astype(BF16))
            m = jnp.maximum(jnp.max(s, axis=-1, keepdims=True), jnp.max(sc, axis=-1, keepdims=True))
        else:
            kw = k_ref[0].astype(BF16)
            vw = v_ref[0].astype(BF16)
            s = _dot_nt(q, kw)
            m = jnp.max(s, axis=-1, keepdims=True)
        p = jnp.exp2(s - m)
        l = jnp.sum(p, axis=-1, keepdims=True)
        acc = jnp.dot(p.astype(BF16), vw, preferred_element_type=F32)
        if windowed:
            pc = jnp.exp2(sc - m)
            l = l + jnp.sum(pc, axis=-1, keepdims=True)
            acc = acc + jnp.dot(pc.astype(BF16), cv_ref[0].astype(BF16), preferred_element_type=F32)
        out = acc / (l + jnp.exp2(sink - m))
        for g in range(G):
            o_ref[0, r0:r0 + tq, g * hd:(g + 1) * hd] = out[g * tq:(g + 1) * tq].astype(o_ref.dtype)


def _gqa(q, q_cb0, k, k_cb0, v, v_cb0, sink, ctx, *, KV, G, hd):
    B, S, _ = q.shape
    windowed = ctx is not None
    tq = Q_BLOCK if windowed else _tile(S, 256, BF16_ROWS)
    nb = max(n for n in (1, 2, 4, 8) if S % (n * tq) == 0) if windowed else 1
    ts = nb * tq
    sink3 = jnp.broadcast_to(sink.astype(F32).reshape(KV, G, 1), (KV, G, LANES))
    in_specs = [
        pl.BlockSpec((1, ts, G * hd), lambda b, h, n: (b, n, q_cb0 + h)),
        pl.BlockSpec((1, S, hd), lambda b, h, n: (b, 0, k_cb0 + h)),
        pl.BlockSpec((1, S, hd), lambda b, h, n: (b, 0, v_cb0 + h)),
        pl.BlockSpec((1, G, LANES), lambda b, h, n: (h, 0, 0)),
    ]
    args = [q, k, v, sink3]
    if windowed:
        ck, cv = ctx
        P = ck.shape[1]
        cspec = pl.BlockSpec((1, P, hd), lambda b, h, n: (b, 0, h))
        in_specs += [cspec, cspec]
        args += [ck, cv]
    return pl.pallas_call(
        functools.partial(_gqa_kernel, G=G, hd=hd, tq=tq, nb=nb, S=S, windowed=windowed),
        grid=(B, KV, S // ts),
        in_specs=in_specs,
        out_specs=pl.BlockSpec((1, ts, G * hd), lambda b, h, n: (b, n, h)),
        out_shape=jax.ShapeDtypeStruct((B, S, KV * G * hd), BF16),
        compiler_params=_params("parallel", "parallel", "parallel"),
        name="gqa_attention",
    )(*args)


POOL_PAD = 16


def _pool_kernel(u_ref, w_ref, ps_ref, o_ref, pad_ref, *, S):
    grp = pl.program_id(1)
    u = u_ref[0].astype(F32)
    C = u.shape[-1]
    pad_ref[0:POOL_PAD, :] = jnp.zeros((POOL_PAD, C), F32)
    pad_ref[POOL_PAD + S:2 * POOL_PAD + S, :] = jnp.zeros((POOL_PAD, C), F32)
    pad_ref[POOL_PAD:POOL_PAD + S, :] = u
    t = lax.broadcasted_iota(jnp.int32, (S, 1), 0)
    for gi, win in enumerate(POOL_WINDOWS):
        @pl.when(grp == gi)
        def _(win=win):
            half = win // 2
            total = pad_ref[POOL_PAD - half:POOL_PAD - half + S, :]
            for off in range(-half + 1, half):
                total = total + pad_ref[POOL_PAD + off:POOL_PAD + off + S, :]
            count = (jnp.clip(t + half, 0, S) - jnp.clip(t - half, 0, S)).astype(F32)
            pooled = (total / count - u).astype(BF16)
            y = jnp.dot(pooled, w_ref[0].astype(BF16), preferred_element_type=F32) * ps_ref[...]
            o_ref[0] = y.astype(o_ref.dtype)


def _pool(u, u_cb0, w_pool, pool_scale):
    B, S, _ = u.shape
    NG, Cg, _ = w_pool.shape
    assert NG == len(POOL_WINDOWS)
    return pl.pallas_call(
        functools.partial(_pool_kernel, S=S),
        grid=(B, NG),
        in_specs=[
            pl.BlockSpec((1, S, Cg), lambda b, g: (b, 0, u_cb0 + g)),
            pl.BlockSpec((1, Cg, Cg), lambda b, g: (g, 0, 0)),
            pl.BlockSpec((1, Cg), lambda b, g: (0, g)),
        ],
        out_specs=pl.BlockSpec((1, S, Cg), lambda b, g: (b, 0, g)),
        out_shape=jax.ShapeDtypeStruct((B, S, NG * Cg), BF16),
        scratch_shapes=[pltpu.VMEM((S + 2 * POOL_PAD, Cg), F32)],
        compiler_params=_params("parallel", "parallel"),
        name="pool_mixer",
    )(u, w_pool, pool_scale.reshape(1, NG * Cg))


def _mla_kernel(qn_ref, qr_ref, kn_ref, kr_ref, v_ref, o_ref, *, dn, hpb):
    tq = qn_ref.shape[1]
    ones = jnp.ones((kn_ref.shape[1], LANES), BF16)
    for r0 in range(0, tq, ATTN_ROWS):
        rows = slice(r0, min(r0 + ATTN_ROWS, tq))
        for hp in range(hpb):
            qr = qr_ref[0, rows, hp * LANES:(hp + 1) * LANES].astype(BF16)
            for e in range(2):
                cols = slice((2 * hp + e) * dn, (2 * hp + e + 1) * dn)
                q = jnp.concatenate([qn_ref[0, rows, cols].astype(BF16), qr], axis=-1)
                k = jnp.concatenate([kn_ref[0, :, cols], kr_ref[0, :, e * LANES:(e + 1) * LANES]], axis=-1)
                s = _dot_nt(q, k)
                p = jnp.exp2(s - jnp.max(s, axis=-1, keepdims=True)).astype(BF16)
                acc = jnp.dot(p, jnp.concatenate([v_ref[0, :, cols], ones], axis=-1), preferred_element_type=F32)
                o_ref[0, rows, cols] = (acc[:, 0:dn] / acc[:, dn:2 * dn]).astype(o_ref.dtype)


def _mla(qn, qn_cb0, qr, qr_cb0, knv, kr2, *, H, dn, dr):
    B, S, _ = qn.shape
    Sk = knv.shape[1]
    assert dn == LANES and 2 * dr == LANES
    tq = _tile(S, ATTN_GROUPS * ATTN_ROWS, BF16_ROWS)
    HP = H // 2
    hpb = _heads_per_step(HP, S, qn_cb0, qr_cb0)
    nh = HP // hpb
    return pl.pallas_call(
        functools.partial(_mla_kernel, dn=dn, hpb=hpb),
        grid=(B, nh, S // tq),
        in_specs=[
            pl.BlockSpec((1, tq, 2 * dn * hpb), lambda b, h, n: (b, n, qn_cb0 // hpb + h)),
            pl.BlockSpec((1, tq, LANES * hpb), lambda b, h, n: (b, n, qr_cb0 // hpb + h)),
            pl.BlockSpec((1, Sk, 2 * dn * hpb), lambda b, h, n: (b, 0, h)),
            pl.BlockSpec((1, Sk, 2 * LANES), lambda b, h, n: (b, 0, 0)),
            pl.BlockSpec((1, Sk, 2 * dn * hpb), lambda b, h, n: (b, 0, nh + h)),
        ],
        out_specs=pl.BlockSpec((1, tq, 2 * dn * hpb), lambda b, h, n: (b, n, h)),
        out_shape=jax.ShapeDtypeStruct((B, S, H * dn), BF16),
        compiler_params=_params("parallel", "parallel", "parallel"),
        name="mla_attention",
    )(qn, qr, knv, kr2, knv)


def _diff_kernel(*refs, dk, lam_init, has_ctx, hb):
    if has_ctx:
        q_ref, k_ref, v_ref, ck_ref, cv_ref, l1_ref, l2_ref, gs_ref, o_ref = refs
        key_refs, val_refs = (k_ref, ck_ref), (v_ref, cv_ref)
    else:
        q_ref, k_ref, v_ref, l1_ref, l2_ref, gs_ref, o_ref = refs
        key_refs, val_refs = (k_ref,), (v_ref,)
    lam = (jnp.exp(jnp.sum(l1_ref[0:1, :] * l1_ref[1:2, :], axis=-1, keepdims=True))
           - jnp.exp(jnp.sum(l2_ref[0:1, :] * l2_ref[1:2, :], axis=-1, keepdims=True)) + lam_init)
    tq = q_ref.shape[1]
    dv = 2 * dk
    for r0 in range(0, tq, ATTN_ROWS):
        rows = slice(r0, min(r0 + ATTN_ROWS, tq))
        for hh in range(hb):
            out = None
            for mi in range(2):
                cols = slice(hh * dv + mi * dk, hh * dv + (mi + 1) * dk)
                q = q_ref[0, rows, cols].astype(BF16)
                ss = [_dot_nt(q, kr[0, :, cols].astype(BF16)) for kr in key_refs]
                m = functools.reduce(jnp.maximum, [jnp.max(s, axis=-1, keepdims=True) for s in ss])
                es = [jnp.exp2(s - m) for s in ss]
                tot = functools.reduce(jnp.add, [jnp.sum(e, axis=-1, keepdims=True) for e in es])
                pv = functools.reduce(jnp.add, [
                    jnp.dot(e.astype(BF16), vr[0, :, hh * dv:(hh + 1) * dv].astype(BF16), preferred_element_type=F32)
                    for e, vr in zip(es, val_refs)])
                out = pv / tot if mi == 0 else out - pv * (lam / tot)
            o_ref[0, rows, hh * dv:(hh + 1) * dv] = (_rms(out, gs_ref[...]) * (1.0 - lam_init)).astype(o_ref.dtype)


def _diff(q, q_cb0, k, k_cb0, v, v_cb0, ctx, lam_q1, lam_k1, lam_q2, lam_k2, g_subln, lam_init, *, H, dk):
    B, S, _ = q.shape
    dv = 2 * dk
    tq = _tile(S, ATTN_GROUPS * ATTN_ROWS, BF16_ROWS)
    l1 = jnp.stack([lam_q1, lam_k1]).astype(F32)
    l2 = jnp.stack([lam_q2, lam_k2]).astype(F32)
    lspec = pl.BlockSpec((2, dk), lambda b, h, n: (0, 0))
    hb = 1 if ctx is not None else _heads_per_step(H, S, q_cb0, k_cb0, v_cb0)
    in_specs = [
        pl.BlockSpec((1, tq, dv * hb), lambda b, h, n: (b, n, q_cb0 // hb + h)),
        pl.BlockSpec((1, S, dv * hb), lambda b, h, n: (b, 0, k_cb0 // hb + h)),
        pl.BlockSpec((1, S, dv * hb), lambda b, h, n: (b, 0, v_cb0 // hb + h)),
    ]
    args = [q, k, v]
    if ctx is not None:
        P = ctx[0].shape[1]
        cspec = pl.BlockSpec((1, P, dv), lambda b, h, n: (b, 0, h))
        in_specs += [cspec, cspec]
        args += list(ctx)
    in_specs += [lspec, lspec, pl.BlockSpec((1, dv), lambda b, h, n: (0, 0))]
    args += [l1, l2, g_subln.reshape(1, dv)]
    return pl.pallas_call(
        functools.partial(_diff_kernel, dk=dk, lam_init=lam_init, has_ctx=ctx is not None, hb=hb),
        grid=(B, H // hb, S // tq),
        in_specs=in_specs,
        out_specs=pl.BlockSpec((1, tq, dv * hb), lambda b, h, n: (b, n, h)),
        out_shape=jax.ShapeDtypeStruct((B, S, H * dv), BF16),
        compiler_params=_params("parallel", "parallel", "parallel"),
        name="diff_attention",
    )(*args)


Z_PAD = 8
FFN_ROW_CHUNK = 32
FFN_DOT_ROWS = 1024


def _ffn_up_kernel(h_ref, wg_ref, wv_ref, cwg_ref, cwv_ref, cbg_ref, cbv_ref, o_ref, z_ref, *, tm, tn, S, rc, rm):
    zero = jnp.zeros((Z_PAD, 2 * tn), F32)
    z_ref[0:Z_PAD, :] = zero
    z_ref[Z_PAD + tm:2 * Z_PAD + tm, :] = zero
    for r0 in range(0, tm, rm):
        h = h_ref[r0:r0 + rm, :]
        z_ref[Z_PAD + r0:Z_PAD + r0 + rm, 0:tn] = jnp.dot(h, wg_ref[...], preferred_element_type=F32)
        z_ref[Z_PAD + r0:Z_PAD + r0 + rm, tn:2 * tn] = jnp.dot(h, wv_ref[...], preferred_element_type=F32)

    cw = jnp.concatenate([cwg_ref[...], cwv_ref[...]], axis=1)
    cb = jnp.concatenate([cbg_ref[...], cbv_ref[...]], axis=1)
    span = rc + 2 * Z_PAD
    for r0 in range(0, tm, rc):
        zf = z_ref[r0:r0 + span, :]
        zc = zf[Z_PAD:Z_PAD + rc]
        zp = pltpu.roll(zf, 1, 0)[Z_PAD:Z_PAD + rc]
        zn = pltpu.roll(zf, span - 1, 0)[Z_PAD:Z_PAD + rc]
        if tm > S:
            pos = (r0 + lax.broadcasted_iota(jnp.int32, (rc, 1), 0)) % S
            zp = jnp.where(pos != 0, zp, 0.0)
            zn = jnp.where(pos != S - 1, zn, 0.0)
        y = zp * cw[0:1, :] + zc * cw[1:2, :] + zn * cw[2:3, :] + cb
        g = y[:, 0:tn]
        sig = 1.0 / (1.0 + jnp.exp2(g * (-LOG2E)))
        o_ref[r0:r0 + rc, :] = (g * sig * y[:, tn:2 * tn]).astype(o_ref.dtype)


def _ffn_up(h, S, w_up, conv_w, conv_b, layer):
    M, D = h.shape
    L = w_up.shape[0]
    F = w_up.shape[2] // 2
    tm = _tile(M, max(2048, S), S)
    assert tm % S == 0
    tn = _tile(F, 256, LANES)
    nj = F // tn
    rc = _tile(tm, FFN_ROW_CHUNK, 8)
    rm = _tile(tm, FFN_DOT_ROWS, BF16_ROWS)
    cb = conv_b.reshape(L, 1, 2 * F)
    gate_tile = lambda i, j: (layer, 0, j)
    val_tile = lambda i, j: (layer, 0, nj + j)
    return pl.pallas_call(
        functools.partial(_ffn_up_kernel, tm=tm, tn=tn, S=S, rc=rc, rm=rm),
        grid=(M // tm, nj),
        in_specs=[
            pl.BlockSpec((tm, D), lambda i, j: (i, 0)),
            pl.BlockSpec((None, D, tn), gate_tile),
            pl.BlockSpec((None, D, tn), val_tile),
            pl.BlockSpec((None, 3, tn), gate_tile),
            pl.BlockSpec((None, 3, tn), val_tile),
            pl.BlockSpec((None, 1, tn), gate_tile),
            pl.BlockSpec((None, 1, tn), val_tile),
        ],
        out_specs=pl.BlockSpec((tm, tn), lambda i, j: (i, j)),
        out_shape=jax.ShapeDtypeStruct((M, F), BF16),
        scratch_shapes=[pltpu.VMEM((tm + 2 * Z_PAD, 2 * tn), F32)],
        compiler_params=_params("parallel", "parallel"),
        name="ffn_up_conv_gate",
    )(h, w_up, w_up, conv_w, conv_w, cb, cb)


def kernel(x_prompt, x_sample, cache_a_k, cache_a_v, cache_c_ckv, cache_c_krope, cache_d_k, cache_d_v, c, c_ctx, w_mod, b_mod, g_pre_mix, g_post_mix, g_pre_ffn, g_post_ffn, w_in_even, w_out_even, a_sink, w_pool, pool_scale, w_in_odd, w_out_odd, g_q_norm, w_uq, g_kv_norm, w_uk, w_uv, lambda_q1, lambda_k1, lambda_q2, lambda_k2, g_subln, w_up, conv_w, conv_b, w_down):
    B1, S1, D = x_prompt.shape
    B2, S2, _ = x_sample.shape
    depth = w_mod.shape[0]
    P = cache_a_k.shape[2]
    KV, hd = cache_a_k.shape[3], cache_a_k.shape[4]
    AH = a_sink.shape[1]
    G = AH // KV
    NG, Cg = w_pool.shape[1], w_pool.shape[2]
    qa, kva, pool_dim = AH * hd, KV * hd, NG * Cg
    CQ = g_q_norm.shape[1]
    CKV, CH, CN = w_uk.shape[1], w_uk.shape[2], w_uk.shape[3]
    CR = cache_c_krope.shape[3]
    CV = w_uv.shape[3]
    DH, DK = cache_d_k.shape[3], cache_d_k.shape[5]
    DV = 2 * DK
    dqk = DH * 2 * DK
    assert hd == LANES and CN == LANES and CV == CN and 2 * CR == LANES and DK == LANES

    R = -(-(B2 + 1) // 8) * 8
    cond = jnp.zeros((R, D), F32).at[:B2].set(c).at[B2].set(c_ctx)
    mods = _mods(cond, w_mod, b_mod).reshape(depth, R, 1, 6 * D)
    SH_M, SC_M, G_M, SH_F, SC_F, G_F = range(6)

    groups = [
        dict(x=x_prompt, B=B1, S=S1, row0=B2, stride=0, latent=False),
        dict(x=x_sample, B=B2, S=S2, row0=0, stride=1, latent=True),
    ]
    rope_hd = _rope_tables(S2, hd)
    rope_cr = _rope_tables(S2, CR)
    w_up_b = w_up.astype(BF16)
    w_down_b = w_down.astype(BF16)

    new_state = {}
    for grp in groups:
        B, S, latent = grp["B"], grp["S"], grp["latent"]
        M = B * S
        row0, stride = grp["row0"], grp["stride"]
        x = grp["x"]
        (h,) = _resid_norm(x, row0, stride, norm=(g_pre_mix[0], mods[0], SC_M, SH_M))
        for i in range(depth):
            j = i // 2
            hf = h.reshape(M, D)
            pdt = BF16 if latent else F32
            proj_tn = WIDE_TILE if latent else 512

            def rope(tables, pair, col0, col1):
                return (tables, pair, col0 // LANES, col1 // LANES, S) if latent else None

            if i % 2 == 0:
                wi = w_in_even[j]
                w_in = jnp.concatenate([wi[:, :qa] * _logit_scale(hd), wi[:, qa:]], axis=1).astype(BF16)
                proj = _matmul(hf, w_in, out_dtype=pdt, tn_pref=proj_tn,
                               rope=rope(rope_hd, hd // 4, 0, qa + kva)).reshape(B, S, -1)
                if latent:
                    ctx = (cache_a_k[:, j].reshape(B, P, kva), cache_a_v[:, j].reshape(B, P, kva))
                else:
                    ctx = None
                    new_state["a_k"] = proj[:, :, qa:qa + kva].reshape(B, 1, S, KV, hd)
                    new_state["a_v"] = proj[:, :, qa + kva:qa + 2 * kva].reshape(B, 1, S, KV, hd)
                a = _gqa(proj, 0, proj, qa // hd, proj, (qa + kva) // hd, a_sink[j], ctx, KV=KV, G=G, hd=hd)
                bmix = _pool(proj, (qa + 2 * kva) // Cg, w_pool[j], pool_scale[j])
                y = _matmul_parts([a.reshape(M, qa), bmix.reshape(M, pool_dim)], w_out_even[j].astype(BF16),
                                  out_dtype=BF16, tn_pref=1024)
            else:
                lam_init = 0.8 - 0.6 * math.exp(-0.3 * i)
                wi = w_in_odd[j]
                c1, c2 = CQ + CKV, CQ + CKV + CR
                w_main = jnp.concatenate([wi[:, :c1], wi[:, c2:c2 + dqk] * _logit_scale(DK), wi[:, c2 + dqk:]],
                                         axis=1).astype(BF16)
                w_kr = jnp.concatenate([wi[:, c1:c2], jnp.zeros((D, LANES - CR), F32)], axis=1).astype(BF16)
                o_dq, o_dk, o_dv = c1, c1 + dqk, c1 + 2 * dqk
                proj = _matmul(hf, w_main, out_dtype=pdt, tn_pref=proj_tn, rope=rope(rope_hd, hd // 4, o_dq, o_dv))
                krp = _matmul(hf, w_kr, out_dtype=pdt, tn_pref=LANES,
                              rope=rope(rope_cr, CR // 4, 0, LANES)).reshape(B, S, LANES)
                (cqn,) = _rms_cols(proj, 0, g_q_norm[j], [BF16])
                ckvn_f, ckvn_b = _rms_cols(proj, CQ // CKV, g_kv_norm[j], [F32, BF16])
                wq = w_uq[j].reshape(CQ, CH, CN + CR)
                wq = jnp.concatenate([wq[:, :, :CN].reshape(CQ, CH * CN), wq[:, :, CN:].reshape(CQ, CH * CR)], axis=1)
                wq = wq * _logit_scale(CN + CR)
                qall = _matmul(cqn, wq.astype(BF16), out_dtype=pdt, tn_pref=1024,
                               rope=rope(rope_cr, CR // 4, CH * CN, CH * (CN + CR))).reshape(B, S, CH * (CN + CR))
                wkv = jnp.concatenate([w_uk[j].reshape(CKV, CH * CN), w_uv[j].reshape(CKV, CH * CV)], axis=1).astype(BF16)
                proj3 = proj.reshape(B, S, -1)
                if latent:
                    ckv_all = jnp.concatenate([ckvn_b.reshape(B, S, CKV), cache_c_ckv[:, j].astype(BF16)], axis=1)
                    kr_all = jnp.concatenate([krp[:, :, :CR], cache_c_krope[:, j].astype(BF16)], axis=1)
                    d_ctx = (cache_d_k[:, j].reshape(B, P, dqk), cache_d_v[:, j].reshape(B, P, DH * DV))
                else:
                    ckv_all = ckvn_b.reshape(B, S, CKV)
                    kr_all = krp[:, :, :CR].astype(BF16)
                    d_ctx = None
                    new_state["c_ckv"] = ckvn_f.reshape(B, 1, S, CKV)
                    new_state["c_krope"] = krp[:, :, :CR].reshape(B, 1, S, CR)
                    new_state["d_k"] = proj3[:, :, o_dk:o_dk + dqk].reshape(B, 1, S, DH, 2, DK)
                    new_state["d_v"] = proj3[:, :, o_dv:o_dv + dqk].reshape(B, 1, S, DH, DV)
                Sk = ckv_all.shape[1]
                knv = _matmul(ckv_all.reshape(B * Sk, CKV), wkv, out_dtype=BF16, tn_pref=2048).reshape(B, Sk, -1)
                zk = jnp.zeros_like(kr_all)
                kr2 = jnp.concatenate([kr_all, zk, zk, kr_all], axis=-1)
                c_out = _mla(qall, 0, qall, CH * CN // LANES, knv, kr2, H=CH, dn=CN, dr=CR)
                d_out = _diff(proj3, o_dq // DV, proj3, o_dk // DV, proj3, o_dv // DV, d_ctx,
                              lambda_q1[j], lambda_k1[j], lambda_q2[j], lambda_k2[j], g_subln[j], lam_init, H=DH, dk=DK)
                y = _matmul_parts([c_out.reshape(M, -1), d_out.reshape(M, -1)], w_out_odd[j].astype(BF16),
                                  out_dtype=BF16, tn_pref=1024)
            x, h = _resid_norm(x, row0, stride, resid=(y, g_post_mix[i], mods[i], G_M),
                               norm=(g_pre_ffn[i], mods[i], SC_F, SH_F))
            hmid = _ffn_up(h.reshape(M, D), S, w_up_b, conv_w, conv_b, i)
            f = _matmul(hmid, w_down_b, layer=i, out_dtype=BF16, tm_pref=512, tn_pref=512)
            if i + 1 < depth:
                x, h = _resid_norm(x, row0, stride, resid=(f, g_post_ffn[i], mods[i], G_F),
                                   norm=(g_pre_mix[i + 1], mods[i + 1], SC_M, SH_M))
            else:
                (x,) = _resid_norm(x, row0, stride, resid=(f, g_post_ffn[i], mods[i], G_F))
        grp["out"] = x

    return (groups[0]["out"], groups[1]["out"], new_state["a_k"], new_state["a_v"], new_state["c_ckv"],
            new_state["c_krope"], new_state["d_k"], new_state["d_v"])
```

```python
import functools
import math

import jax
import jax.numpy as jnp
from jax import lax
from jax.experimental import pallas as pl
from jax.experimental.pallas import tpu as pltpu

GRID_W = 64
Q_BLOCK = 128
WINDOW = 128
ROPE_BASE = 10000.0
EPS = 1e-6
NEG_INF = -1e30
POOL_WINDOWS = (2, 4, 8, 16)

LANES = 128
BF16_ROWS = 16
VMEM_LIMIT = 58 * 1024 * 1024
WIDE_TILE = 11 * LANES

BF16 = jnp.bfloat16
F32 = jnp.float32


def _tile(dim, pref, align):
    best = None
    t = align
    while t <= min(dim, pref):
        if dim % t == 0:
            best = t
        t += align
    return dim if best is None else best


def _params(*sem):
    return pltpu.CompilerParams(dimension_semantics=sem, vmem_limit_bytes=VMEM_LIMIT)


def _mods_kernel(c_ref, w_ref, b_ref, o_ref):
    c = c_ref[...]
    s = (c * (1.0 / (1.0 + jnp.exp(-c)))).astype(BF16)
    o_ref[...] = jnp.dot(s, w_ref[...].astype(BF16), preferred_element_type=F32) + b_ref[...]


def _mods(cond, w_mod, b_mod):
    depth, D, N = w_mod.shape
    R = cond.shape[0]
    tn = _tile(N, 512, LANES)
    return pl.pallas_call(
        _mods_kernel,
        grid=(depth, N // tn),
        in_specs=[
            pl.BlockSpec((R, D), lambda l, j: (0, 0)),
            pl.BlockSpec((None, D, tn), lambda l, j: (l, 0, j)),
            pl.BlockSpec((None, 1, tn), lambda l, j: (l, 0, j)),
        ],
        out_specs=pl.BlockSpec((None, R, tn), lambda l, j: (l, 0, j)),
        out_shape=jax.ShapeDtypeStruct((depth, R, N), F32),
        compiler_params=_params("parallel", "parallel"),
        name="mods",
    )(cond, w_mod, b_mod.reshape(depth, 1, N))


def _rms(x, g):
    return x * lax.rsqrt(jnp.mean(x * x, axis=-1, keepdims=True) + EPS) * g


def _resid_norm_kernel(*refs, has_resid, has_norm):
    refs = list(refs)
    x = refs.pop(0)[0]
    if has_resid:
        y_ref, gpost_ref, gate_ref = refs[:3]
        refs = refs[3:]
        x = x + gate_ref[...] * _rms(y_ref[0].astype(F32), gpost_ref[...])
    if has_norm:
        gpre_ref, sc_ref, sh_ref = refs[:3]
        refs = refs[3:]
    if has_resid:
        refs.pop(0)[0] = x
    if has_norm:
        h = _rms(x, gpre_ref[...]) * (1.0 + sc_ref[...]) + sh_ref[...]
        refs.pop(0)[0] = h.astype(BF16)


def _resid_norm(x, row0, row_stride, *, resid=None, norm=None):
    B, S, D = x.shape
    ts = _tile(S, 512, 8)
    xspec = pl.BlockSpec((1, ts, D), lambda b, s: (b, s, 0))
    gspec = pl.BlockSpec((1, D), lambda b, s: (0, 0))

    def mspec(col):
        return pl.BlockSpec((None, 1, D), lambda b, s: (row0 + b * row_stride, 0, col))

    args, in_specs, out_shape, out_specs = [x], [xspec], [], []
    if resid is not None:
        y, gpost, mods, gate_col = resid
        args += [y.reshape(B, S, D), gpost.reshape(1, D), mods]
        in_specs += [xspec, gspec, mspec(gate_col)]
        out_shape.append(jax.ShapeDtypeStruct((B, S, D), F32))
        out_specs.append(xspec)
    if norm is not None:
        gpre, mods, sc_col, sh_col = norm
        args += [gpre.reshape(1, D), mods, mods]
        in_specs += [gspec, mspec(sc_col), mspec(sh_col)]
        out_shape.append(jax.ShapeDtypeStruct((B, S, D), BF16))
        out_specs.append(xspec)
    outs = pl.pallas_call(
        functools.partial(_resid_norm_kernel, has_resid=resid is not None, has_norm=norm is not None),
        grid=(B, S // ts),
        in_specs=in_specs,
        out_specs=out_specs,
        out_shape=out_shape,
        compiler_params=_params("parallel", "parallel"),
        name="resid_norm",
    )(*args)
    return outs


def _mm_kernel(x_ref, w_ref, o_ref):
    o_ref[...] = jnp.dot(x_ref[...].astype(BF16), w_ref[...], preferred_element_type=F32).astype(o_ref.dtype)


def _mm_parts_kernel(*refs):
    o_ref = refs[-1]
    acc = None
    for x_ref, w_ref in zip(refs[0:-1:2], refs[1:-1:2]):
        d = jnp.dot(x_ref[...], w_ref[...], preferred_element_type=F32)
        acc = d if acc is None else acc + d
    o_ref[...] = acc.astype(o_ref.dtype)


def _matmul_parts(xs, w, *, out_dtype=F32, tm_pref=1024, tn_pref=512):
    M = xs[0].shape[0]
    N = w.shape[1]
    starts = [sum(x.shape[1] for x in xs[:n]) for n in range(len(xs))]
    if any(k0 % x.shape[1] for k0, x in zip(starts, xs)):
        return _matmul(jnp.concatenate(xs, axis=1), w, out_dtype=out_dtype, tm_pref=tm_pref, tn_pref=tn_pref)
    tm = _tile(M, tm_pref, BF16_ROWS)
    tn = _tile(N, tn_pref, LANES)
    in_specs, args, k0 = [], [], 0
    for x in xs:
        kp = x.shape[1]
        in_specs += [pl.BlockSpec((tm, kp), lambda i, j: (i, 0)),
                     pl.BlockSpec((kp, tn), lambda i, j, kb=k0 // kp: (kb, j))]
        args += [x, w]
        k0 += kp
    return pl.pallas_call(
        _mm_parts_kernel,
        grid=(M // tm, N // tn),
        in_specs=in_specs,
        out_specs=pl.BlockSpec((tm, tn), lambda i, j: (i, j)),
        out_shape=jax.ShapeDtypeStruct((M, N), out_dtype),
        compiler_params=_params("parallel", "parallel"),
        name="matmul_parts",
    )(*args)


def _rotate_slab(x, cos, sin, first, pair):
    partner = jnp.where(first, pltpu.roll(x, LANES - pair, 1), pltpu.roll(x, pair, 1))
    return x * cos + partner * sin


def _mm_rope_kernel(x_ref, w_ref, cos_ref, sin_ref, o_ref, *, pair, g0, g1):
    tm, tn = o_ref.shape
    spt = tn // LANES
    acc = jnp.dot(x_ref[...].astype(BF16), w_ref[...], preferred_element_type=F32)
    j = pl.program_id(1)
    lane = lax.broadcasted_iota(jnp.int32, (tm, LANES), 1)
    first = (lane % (2 * pair)) < pair

    def emit(rotated):
        for c in range(spt):
            x = acc[:, c * LANES:(c + 1) * LANES]
            if rotated[c]:
                x = _rotate_slab(x, cos_ref[...], sin_ref[...], first, pair)
            o_ref[:, c * LANES:(c + 1) * LANES] = x.astype(o_ref.dtype)

    full_lo, full_hi = -(-g0 // spt), g1 // spt
    mixed = [t for t in {g0 // spt, (g1 - 1) // spt} if not full_lo <= t < full_hi]
    is_full = (j >= full_lo) & (j < full_hi)
    is_mixed = functools.reduce(jnp.logical_or, [j == t for t in mixed], jnp.bool_(False))
    pl.when(is_full)(lambda: emit([True] * spt))
    for t in mixed:
        pl.when(j == t)(lambda t=t: emit([g0 <= t * spt + c < g1 for c in range(spt)]))
    pl.when(jnp.logical_not(is_full | is_mixed))(lambda: emit([False] * spt))


def _matmul(x, w, *, layer=None, out_dtype=F32, tm_pref=1024, tn_pref=512, rope=None):
    M, K = x.shape
    N = w.shape[-1]
    tn = _tile(N, tn_pref, LANES)
    if layer is None:
        wspec = pl.BlockSpec((K, tn), lambda i, j: (0, j))
    else:
        wspec = pl.BlockSpec((None, K, tn), lambda i, j: (layer, 0, j))
    if rope is None:
        tm = _tile(M, tm_pref, BF16_ROWS)
        kern, extra_specs, extra_args = _mm_kernel, [], []
    else:
        (cos, sin), pair, g0, g1, S = rope
        tm = _tile(S, tm_pref, BF16_ROWS)
        tps = S // tm
        tspec = pl.BlockSpec((tm, LANES), lambda i, j: (i % tps, 0))
        kern = functools.partial(_mm_rope_kernel, pair=pair, g0=g0, g1=g1)
        extra_specs, extra_args = [tspec, tspec], [cos, sin]
    return pl.pallas_call(
        kern,
        grid=(M // tm, N // tn),
        in_specs=[
            pl.BlockSpec((tm, K), lambda i, j: (i, 0)),
            wspec,
        ] + extra_specs,
        out_specs=pl.BlockSpec((tm, tn), lambda i, j: (i, j)),
        out_shape=jax.ShapeDtypeStruct((M, N), out_dtype),
        compiler_params=_params("parallel", "parallel"),
        name="matmul",
    )(x, w, *extra_args)


def _rms_cols_kernel(x_ref, g_ref, *o_refs):
    y = _rms(x_ref[...].astype(F32), g_ref[...])
    for o_ref in o_refs:
        o_ref[...] = y.astype(o_ref.dtype)


def _rms_cols(x, col_block, g, out_dtypes):
    M = x.shape[0]
    W = g.shape[-1]
    tm = _tile(M, 1024, BF16_ROWS)
    spec = pl.BlockSpec((tm, W), lambda i: (i, 0))
    return pl.pallas_call(
        _rms_cols_kernel,
        grid=(M // tm,),
        in_specs=[pl.BlockSpec((tm, W), lambda i: (i, col_block)), pl.BlockSpec((1, W), lambda i: (0, 0))],
        out_specs=[spec] * len(out_dtypes),
        out_shape=[jax.ShapeDtypeStruct((M, W), dt) for dt in out_dtypes],
        compiler_params=_params("parallel"),
        name="rms_cols",
    )(x, g.reshape(1, W))


def _rope_tables(S, width):
    d = width // 2
    half = d // 2
    inv = ROPE_BASE ** (-jnp.arange(half, dtype=F32) / half)
    t = jnp.arange(S)
    row = (t // GRID_W).astype(F32)[:, None] * inv[None, :]
    col = (t % GRID_W).astype(F32)[:, None] * inv[None, :]
    cos = jnp.concatenate([jnp.cos(row), jnp.cos(row), jnp.cos(col), jnp.cos(col)], axis=-1)
    sin = jnp.concatenate([-jnp.sin(row), jnp.sin(row), -jnp.sin(col), jnp.sin(col)], axis=-1)
    reps = LANES // width
    return jnp.tile(cos, (1, reps)), jnp.tile(sin, (1, reps))


LOG2E = math.log2(math.e)
ATTN_ROWS = 256
ATTN_GROUPS = 8


def _heads_per_step(n_heads, S, *col_block_offsets):
    if S > ATTN_ROWS:
        return 1
    return max(d for d in range(1, n_heads + 1)
               if n_heads % d == 0 and all(off % d == 0 for off in col_block_offsets))


def _logit_scale(d):
    return d ** -0.5 * LOG2E


def _dot_nt(a, b):
    return lax.dot_general(a, b, (((1,), (1,)), ((), ())), preferred_element_type=F32)


def _gqa_kernel(*refs, G, hd, tq, nb, S, windowed):
    if windowed:
        q_ref, k_ref, v_ref, sink_ref, ck_ref, cv_ref, o_ref = refs
    else:
        q_ref, k_ref, v_ref, sink_ref, o_ref = refs
    rows = G * tq
    sink = jnp.concatenate([jnp.broadcast_to(sink_ref[0, g:g + 1, :], (tq, LANES)) for g in range(G)], axis=0) * LOG2E
    if windowed:
        span = 3 * Q_BLOCK
        rel0 = (lax.broadcasted_iota(jnp.int32, (rows, span), 1)
                - lax.broadcasted_iota(jnp.int32, (rows, span), 0) % tq)
    for blk in range(nb):
        r0 = blk * tq
        q = jnp.concatenate([q_ref[0, r0:r0 + tq, g * hd:(g + 1) * hd] for g in range(G)], axis=0).astype(BF16)
        if windowed:
            n = pl.program_id(2) * nb + blk
            start = pl.multiple_of(jnp.clip(n * tq - Q_BLOCK, 0, S - span), Q_BLOCK)
            kw = k_ref[0, pl.ds(start, span), :].astype(BF16)
            vw = v_ref[0, pl.ds(start, span), :].astype(BF16)
            s = jnp.where(jnp.abs(rel0 + (start - n * tq)) <= WINDOW, _dot_nt(q, kw), NEG_INF)
            sc = _dot_nt(q, ck_ref[0].astype(BF16))
            m = jnp.maximum(jnp.max(s, axis=-1, keepdims=True), jnp.max(sc, axis=-1, keepdims=True))
        else:
            kw = k_ref[0].astype(BF16)
            vw = v_ref[0].astype(BF16)
            s = _dot_nt(q, kw)
            m = jnp.max(s, axis=-1, keepdims=True)
        p = jnp.exp2(s - m)
        l = jnp.sum(p, axis=-1, keepdims=True)
        acc = jnp.dot(p.astype(BF16), vw, preferred_element_type=F32)
        if windowed:
            pc = jnp.exp2(sc - m)
            l = l + jnp.sum(pc, axis=-1, keepdims=True)
            acc = acc + jnp.dot(pc.astype(BF16), cv_ref[0].astype(BF16), preferred_element_type=F32)
        out = acc / (l + jnp.exp2(sink - m))
        for g in range(G):
            o_ref[0, r0:r0 + tq, g * hd:(g + 1) * hd] = out[g * tq:(g + 1) * tq].astype(o_ref.dtype)


def _gqa(q, q_cb0, k, k_cb0, v, v_cb0, sink, ctx, *, KV, G, hd):
    B, S, _ = q.shape
    windowed = ctx is not None
    tq = Q_BLOCK if windowed else _tile(S, 256, BF16_ROWS)
    nb = max(n for n in (1, 2, 4, 8) if S % (n * tq) == 0) if windowed else 1
    ts = nb * tq
    sink3 = jnp.broadcast_to(sink.astype(F32).reshape(KV, G, 1), (KV, G, LANES))
    in_specs = [
        pl.BlockSpec((1, ts, G * hd), lambda b, h, n: (b, n, q_cb0 + h)),
        pl.BlockSpec((1, S, hd), lambda b, h, n: (b, 0, k_cb0 + h)),
        pl.BlockSpec((1, S, hd), lambda b, h, n: (b, 0, v_cb0 + h)),
        pl.BlockSpec((1, G, LANES), lambda b, h, n: (h, 0, 0)),
    ]
    args = [q, k, v, sink3]
    if windowed:
        ck, cv = ctx
        P = ck.shape[1]
        cspec = pl.BlockSpec((1, P, hd), lambda b, h, n: (b, 0, h))
        in_specs += [cspec, cspec]
        args += [ck, cv]
    return pl.pallas_call(
        functools.partial(_gqa_kernel, G=G, hd=hd, tq=tq, nb=nb, S=S, windowed=windowed),
        grid=(B, KV, S // ts),
        in_specs=in_specs,
        out_specs=pl.BlockSpec((1, ts, G * hd), lambda b, h, n: (b, n, h)),
        out_shape=jax.ShapeDtypeStruct((B, S, KV * G * hd), BF16),
        compiler_params=_params("parallel", "parallel", "parallel"),
        name="gqa_attention",
    )(*args)


POOL_PAD = 16


def _pool_kernel(u_ref, w_ref, ps_ref, o_ref, pad_ref, *, S):
    grp = pl.program_id(1)
    u = u_ref[0].astype(F32)
    C = u.shape[-1]
    pad_ref[0:POOL_PAD, :] = jnp.zeros((POOL_PAD, C), F32)
    pad_ref[POOL_PAD + S:2 * POOL_PAD + S, :] = jnp.zeros((POOL_PAD, C), F32)
    pad_ref[POOL_PAD:POOL_PAD + S, :] = u
    t = lax.broadcasted_iota(jnp.int32, (S, 1), 0)
    for gi, win in enumerate(POOL_WINDOWS):
        @pl.when(grp == gi)
        def _(win=win):
            half = win // 2
            total = pad_ref[POOL_PAD - half:POOL_PAD - half + S, :]
            for off in range(-half + 1, half):
                total = total + pad_ref[POOL_PAD + off:POOL_PAD + off + S, :]
            count = (jnp.clip(t + half, 0, S) - jnp.clip(t - half, 0, S)).astype(F32)
            pooled = (total / count - u).astype(BF16)
            y = jnp.dot(pooled, w_ref[0].astype(BF16), preferred_element_type=F32) * ps_ref[...]
            o_ref[0] = y.astype(o_ref.dtype)


def _pool(u, u_cb0, w_pool, pool_scale):
    B, S, _ = u.shape
    NG, Cg, _ = w_pool.shape
    assert NG == len(POOL_WINDOWS)
    return pl.pallas_call(
        functools.partial(_pool_kernel, S=S),
        grid=(B, NG),
        in_specs=[
            pl.BlockSpec((1, S, Cg), lambda b, g: (b, 0, u_cb0 + g)),
            pl.BlockSpec((1, Cg, Cg), lambda b, g: (g, 0, 0)),
            pl.BlockSpec((1, Cg), lambda b, g: (0, g)),
        ],
        out_specs=pl.BlockSpec((1, S, Cg), lambda b, g: (b, 0, g)),
        out_shape=jax.ShapeDtypeStruct((B, S, NG * Cg), BF16),
        scratch_shapes=[pltpu.VMEM((S + 2 * POOL_PAD, Cg), F32)],
        compiler_params=_params("parallel", "parallel"),
        name="pool_mixer",
    )(u, w_pool, pool_scale.reshape(1, NG * Cg))


def _mla_kernel(qn_ref, qr_ref, kn_ref, kr_ref, v_ref, o_ref, *, dn, hpb):
    tq = qn_ref.shape[1]
    ones = jnp.ones((kn_ref.shape[1], LANES), BF16)
    for r0 in range(0, tq, ATTN_ROWS):
        rows = slice(r0, min(r0 + ATTN_ROWS, tq))
        for hp in range(hpb):
            qr = qr_ref[0, rows, hp * LANES:(hp + 1) * LANES].astype(BF16)
            for e in range(2):
                cols = slice((2 * hp + e) * dn, (2 * hp + e + 1) * dn)
                q = jnp.concatenate([qn_ref[0, rows, cols].astype(BF16), qr], axis=-1)
                k = jnp.concatenate([kn_ref[0, :, cols], kr_ref[0, :, e * LANES:(e + 1) * LANES]], axis=-1)
                s = _dot_nt(q, k)
                p = jnp.exp2(s - jnp.max(s, axis=-1, keepdims=True)).astype(BF16)
                acc = jnp.dot(p, jnp.concatenate([v_ref[0, :, cols], ones], axis=-1), preferred_element_type=F32)
                o_ref[0, rows, cols] = (acc[:, 0:dn] / acc[:, dn:2 * dn]).astype(o_ref.dtype)


def _mla(qn, qn_cb0, qr, qr_cb0, knv, kr2, *, H, dn, dr):
    B, S, _ = qn.shape
    Sk = knv.shape[1]
    assert dn == LANES and 2 * dr == LANES
    tq = _tile(S, ATTN_GROUPS * ATTN_ROWS, BF16_ROWS)
    HP = H // 2
    hpb = _heads_per_step(HP, S, qn_cb0, qr_cb0)
    nh = HP // hpb
    return pl.pallas_call(
        functools.partial(_mla_kernel, dn=dn, hpb=hpb),
        grid=(B, nh, S // tq),
        in_specs=[
            pl.BlockSpec((1, tq, 2 * dn * hpb), lambda b, h, n: (b, n, qn_cb0 // hpb + h)),
            pl.BlockSpec((1, tq, LANES * hpb), lambda b, h, n: (b, n, qr_cb0 // hpb + h)),
            pl.BlockSpec((1, Sk, 2 * dn * hpb), lambda b, h, n: (b, 0, h)),
            pl.BlockSpec((1, Sk, 2 * LANES), lambda b, h, n: (b, 0, 0)),
            pl.BlockSpec((1, Sk, 2 * dn * hpb), lambda b, h, n: (b, 0, nh + h)),
        ],
        out_specs=pl.BlockSpec((1, tq, 2 * dn * hpb), lambda b, h, n: (b, n, h)),
        out_shape=jax.ShapeDtypeStruct((B, S, H * dn), BF16),
        compiler_params=_params("parallel", "parallel", "parallel"),
        name="mla_attention",
    )(qn, qr, knv, kr2, knv)


def _diff_kernel(*refs, dk, lam_init, has_ctx, hb):
    if has_ctx:
        q_ref, k_ref, v_ref, ck_ref, cv_ref, l1_ref, l2_ref, gs_ref, o_ref = refs
        key_refs, val_refs = (k_ref, ck_ref), (v_ref, cv_ref)
    else:
        q_ref, k_ref, v_ref, l1_ref, l2_ref, gs_ref, o_ref = refs
        key_refs, val_refs = (k_ref,), (v_ref,)
    lam = (jnp.exp(jnp.sum(l1_ref[0:1, :] * l1_ref[1:2, :], axis=-1, keepdims=True))
           - jnp.exp(jnp.sum(l2_ref[0:1, :] * l2_ref[1:2, :], axis=-1, keepdims=True)) + lam_init)
    tq = q_ref.shape[1]
    dv = 2 * dk
    for r0 in range(0, tq, ATTN_ROWS):
        rows = slice(r0, min(r0 + ATTN_ROWS, tq))
        for hh in range(hb):
            out = None
            for mi in range(2):
                cols = slice(hh * dv + mi * dk, hh * dv + (mi + 1) * dk)
                q = q_ref[0, rows, cols].astype(BF16)
                ss = [_dot_nt(q, kr[0, :, cols].astype(BF16)) for kr in key_refs]
                m = functools.reduce(jnp.maximum, [jnp.max(s, axis=-1, keepdims=True) for s in ss])
                es = [jnp.exp2(s - m) for s in ss]
                tot = functools.reduce(jnp.add, [jnp.sum(e, axis=-1, keepdims=True) for e in es])
                pv = functools.reduce(jnp.add, [
                    jnp.dot(e.astype(BF16), vr[0, :, hh * dv:(hh + 1) * dv].astype(BF16), preferred_element_type=F32)
                    for e, vr in zip(es, val_refs)])
                out = pv / tot if mi == 0 else out - pv * (lam / tot)
            o_ref[0, rows, hh * dv:(hh + 1) * dv] = (_rms(out, gs_ref[...]) * (1.0 - lam_init)).astype(o_ref.dtype)


def _diff(q, q_cb0, k, k_cb0, v, v_cb0, ctx, lam_q1, lam_k1, lam_q2, lam_k2, g_subln, lam_init, *, H, dk):
    B, S, _ = q.shape
    dv = 2 * dk
    tq = _tile(S, ATTN_GROUPS * ATTN_ROWS, BF16_ROWS)
    l1 = jnp.stack([lam_q1, lam_k1]).astype(F32)
    l2 = jnp.stack([lam_q2, lam_k2]).astype(F32)
    lspec = pl.BlockSpec((2, dk), lambda b, h, n: (0, 0))
    hb = 1 if ctx is not None else _heads_per_step(H, S, q_cb0, k_cb0, v_cb0)
    in_specs = [
        pl.BlockSpec((1, tq, dv * hb), lambda b, h, n: (b, n, q_cb0 // hb + h)),
        pl.BlockSpec((1, S, dv * hb), lambda b, h, n: (b, 0, k_cb0 // hb + h)),
        pl.BlockSpec((1, S, dv * hb), lambda b, h, n: (b, 0, v_cb0 // hb + h)),
    ]
    args = [q, k, v]
    if ctx is not None:
        P = ctx[0].shape[1]
        cspec = pl.BlockSpec((1, P, dv), lambda b, h, n: (b, 0, h))
        in_specs += [cspec, cspec]
        args += list(ctx)
    in_specs += [lspec, lspec, pl.BlockSpec((1, dv), lambda b, h, n: (0, 0))]
    args += [l1, l2, g_subln.reshape(1, dv)]
    return pl.pallas_call(
        functools.partial(_diff_kernel, dk=dk, lam_init=lam_init, has_ctx=ctx is not None, hb=hb),
        grid=(B, H // hb, S // tq),
        in_specs=in_specs,
        out_specs=pl.BlockSpec((1, tq, dv * hb), lambda b, h, n: (b, n, h)),
        out_shape=jax.ShapeDtypeStruct((B, S, H * dv), BF16),
        compiler_params=_params("parallel", "parallel", "parallel"),
        name="diff_attention",
    )(*args)


Z_PAD = 8
FFN_ROW_CHUNK = 32
FFN_DOT_ROWS = 1024


def _ffn_up_kernel(h_ref, wg_ref, wv_ref, cwg_ref, cwv_ref, cbg_ref, cbv_ref, o_ref, z_ref, *, tm, tn, S, rc, rm):
    zero = jnp.zeros((Z_PAD, 2 * tn), F32)
    z_ref[0:Z_PAD, :] = zero
    z_ref[Z_PAD + tm:2 * Z_PAD + tm, :] = zero
    for r0 in range(0, tm, rm):
        h = h_ref[r0:r0 + rm, :]
        z_ref[Z_PAD + r0:Z_PAD + r0 + rm, 0:tn] = jnp.dot(h, wg_ref[...], preferred_element_type=F32)
        z_ref[Z_PAD + r0:Z_PAD + r0 + rm, tn:2 * tn] = jnp.dot(h, wv_ref[...], preferred_element_type=F32)

    cw = jnp.concatenate([cwg_ref[...], cwv_ref[...]], axis=1)
    cb = jnp.concatenate([cbg_ref[...], cbv_ref[...]], axis=1)
    span = rc + 2 * Z_PAD
    for r0 in range(0, tm, rc):
        zf = z_ref[r0:r0 + span, :]
        zc = zf[Z_PAD:Z_PAD + rc]
        zp = pltpu.roll(zf, 1, 0)[Z_PAD:Z_PAD + rc]
        zn = pltpu.roll(zf, span - 1, 0)[Z_PAD:Z_PAD + rc]
        if tm > S:
            pos = (r0 + lax.broadcasted_iota(jnp.int32, (rc, 1), 0)) % S
            zp = jnp.where(pos != 0, zp, 0.0)
            zn = jnp.where(pos != S - 1, zn, 0.0)
        y = zp * cw[0:1, :] + zc * cw[1:2, :] + zn * cw[2:3, :] + cb
        g = y[:, 0:tn]
        sig = 1.0 / (1.0 + jnp.exp2(g * (-LOG2E)))
        o_ref[r0:r0 + rc, :] = (g * sig * y[:, tn:2 * tn]).astype(o_ref.dtype)


def _ffn_up(h, S, w_up, conv_w, conv_b, layer):
    M, D = h.shape
    L = w_up.shape[0]
    F = w_up.shape[2] // 2
    tm = _tile(M, max(2048, S), S)
    assert tm % S == 0
    tn = _tile(F, 256, LANES)
    nj = F // tn
    rc = _tile(tm, FFN_ROW_CHUNK, 8)
    rm = _tile(tm, FFN_DOT_ROWS, BF16_ROWS)
    cb = conv_b.reshape(L, 1, 2 * F)
    gate_tile = lambda i, j: (layer, 0, j)
    val_tile = lambda i, j: (layer, 0, nj + j)
    return pl.pallas_call(
        functools.partial(_ffn_up_kernel, tm=tm, tn=tn, S=S, rc=rc, rm=rm),
        grid=(M // tm, nj),
        in_specs=[
            pl.BlockSpec((tm, D), lambda i, j: (i, 0)),
            pl.BlockSpec((None, D, tn), gate_tile),
            pl.BlockSpec((None, D, tn), val_tile),
            pl.BlockSpec((None, 3, tn), gate_tile),
            pl.BlockSpec((None, 3, tn), val_tile),
            pl.BlockSpec((None, 1, tn), gate_tile),
            pl.BlockSpec((None, 1, tn), val_tile),
        ],
        out_specs=pl.BlockSpec((tm, tn), lambda i, j: (i, j)),
        out_shape=jax.ShapeDtypeStruct((M, F), BF16),
        scratch_shapes=[pltpu.VMEM((tm + 2 * Z_PAD, 2 * tn), F32)],
        compiler_params=_params("parallel", "parallel"),
        name="ffn_up_conv_gate",
    )(h, w_up, w_up, conv_w, conv_w, cb, cb)


def kernel(x_prompt, x_sample, cache_a_k, cache_a_v, cache_c_ckv, cache_c_krope, cache_d_k, cache_d_v, c, c_ctx, w_mod, b_mod, g_pre_mix, g_post_mix, g_pre_ffn, g_post_ffn, w_in_even, w_out_even, a_sink, w_pool, pool_scale, w_in_odd, w_out_odd, g_q_norm, w_uq, g_kv_norm, w_uk, w_uv, lambda_q1, lambda_k1, lambda_q2, lambda_k2, g_subln, w_up, conv_w, conv_b, w_down):
    B1, S1, D = x_prompt.shape
    B2, S2, _ = x_sample.shape
    depth = w_mod.shape[0]
    P = cache_a_k.shape[2]
    KV, hd = cache_a_k.shape[3], cache_a_k.shape[4]
    AH = a_sink.shape[1]
    G = AH // KV
    NG, Cg = w_pool.shape[1], w_pool.shape[2]
    qa, kva, pool_dim = AH * hd, KV * hd, NG * Cg
    CQ = g_q_norm.shape[1]
    CKV, CH, CN = w_uk.shape[1], w_uk.shape[2], w_uk.shape[3]
    CR = cache_c_krope.shape[3]
    CV = w_uv.shape[3]
    DH, DK = cache_d_k.shape[3], cache_d_k.shape[5]
    DV = 2 * DK
    dqk = DH * 2 * DK
    assert hd == LANES and CN == LANES and CV == CN and 2 * CR == LANES and DK == LANES

    R = -(-(B2 + 1) // 8) * 8
    cond = jnp.zeros((R, D), F32).at[:B2].set(c).at[B2].set(c_ctx)
    mods = _mods(cond, w_mod, b_mod).reshape(depth, R, 1, 6 * D)
    SH_M, SC_M, G_M, SH_F, SC_F, G_F = range(6)

    groups = [
        dict(x=x_prompt, B=B1, S=S1, row0=B2, stride=0, latent=False),
        dict(x=x_sample, B=B2, S=S2, row0=0, stride=1, latent=True),
    ]
    rope_hd = _rope_tables(S2, hd)
    rope_cr = _rope_tables(S2, CR)
    w_up_b = w_up.astype(BF16)
    w_down_b = w_down.astype(BF16)

    new_state = {}
    for grp in groups:
        B, S, latent = grp["B"], grp["S"], grp["latent"]
        M = B * S
        row0, stride = grp["row0"], grp["stride"]
        x = grp["x"]
        (h,) = _resid_norm(x, row0, stride, norm=(g_pre_mix[0], mods[0], SC_M, SH_M))
        for i in range(depth):
            j = i // 2
            hf = h.reshape(M, D)
            pdt = BF16 if latent else F32
            proj_tn = WIDE_TILE if latent else 512

            def rope(tables, pair, col0, col1):
                return (tables, pair, col0 // LANES, col1 // LANES, S) if latent else None

            if i % 2 == 0:
                wi = w_in_even[j]
                w_in = jnp.concatenate([wi[:, :qa] * _logit_scale(hd), wi[:, qa:]], axis=1).astype(BF16)
                proj = _matmul(hf, w_in, out_dtype=pdt, tn_pref=proj_tn,
                               rope=rope(rope_hd, hd // 4, 0, qa + kva)).reshape(B, S, -1)
                if latent:
                    ctx = (cache_a_k[:, j].reshape(B, P, kva), cache_a_v[:, j].reshape(B, P, kva))
                else:
                    ctx = None
                    new_state["a_k"] = proj[:, :, qa:qa + kva].reshape(B, 1, S, KV, hd)
                    new_state["a_v"] = proj[:, :, qa + kva:qa + 2 * kva].reshape(B, 1, S, KV, hd)
                a = _gqa(proj, 0, proj, qa // hd, proj, (qa + kva) // hd, a_sink[j], ctx, KV=KV, G=G, hd=hd)
                bmix = _pool(proj, (qa + 2 * kva) // Cg, w_pool[j], pool_scale[j])
                y = _matmul_parts([a.reshape(M, qa), bmix.reshape(M, pool_dim)], w_out_even[j].astype(BF16),
                                  out_dtype=BF16, tn_pref=1024)
            else:
                lam_init = 0.8 - 0.6 * math.exp(-0.3 * i)
                wi = w_in_odd[j]
                c1, c2 = CQ + CKV, CQ + CKV + CR
                w_main = jnp.concatenate([wi[:, :c1], wi[:, c2:c2 + dqk] * _logit_scale(DK), wi[:, c2 + dqk:]],
                                         axis=1).astype(BF16)
                w_kr = jnp.concatenate([wi[:, c1:c2], jnp.zeros((D, LANES - CR), F32)], axis=1).astype(BF16)
                o_dq, o_dk, o_dv = c1, c1 + dqk, c1 + 2 * dqk
                proj = _matmul(hf, w_main, out_dtype=pdt, tn_pref=proj_tn, rope=rope(rope_hd, hd // 4, o_dq, o_dv))
                krp = _matmul(hf, w_kr, out_dtype=pdt, tn_pref=LANES,
                              rope=rope(rope_cr, CR // 4, 0, LANES)).reshape(B, S, LANES)
                (cqn,) = _rms_cols(proj, 0, g_q_norm[j], [BF16])
                ckvn_f, ckvn_b = _rms_cols(proj, CQ // CKV, g_kv_norm[j], [F32, BF16])
                wq = w_uq[j].reshape(CQ, CH, CN + CR)
                wq = jnp.concatenate([wq[:, :, :CN].reshape(CQ, CH * CN), wq[:, :, CN:].reshape(CQ, CH * CR)], axis=1)
                wq = wq * _logit_scale(CN + CR)
                qall = _matmul(cqn, wq.astype(BF16), out_dtype=pdt, tn_pref=1024,
                               rope=rope(rope_cr, CR // 4, CH * CN, CH * (CN + CR))).reshape(B, S, CH * (CN + CR))
                wkv = jnp.concatenate([w_uk[j].reshape(CKV, CH * CN), w_uv[j].reshape(CKV, CH * CV)], axis=1).astype(BF16)
                proj3 = proj.reshape(B, S, -1)
                if latent:
                    ckv_all = jnp.concatenate([ckvn_b.reshape(B, S, CKV), cache_c_ckv[:, j].astype(BF16)], axis=1)
                    kr_all = jnp.concatenate([krp[:, :, :CR], cache_c_krope[:, j].astype(BF16)], axis=1)
                    d_ctx = (cache_d_k[:, j].reshape(B, P, dqk), cache_d_v[:, j].reshape(B, P, DH * DV))
                else:
                    ckv_all = ckvn_b.reshape(B, S, CKV)
                    kr_all = krp[:, :, :CR].astype(BF16)
                    d_ctx = None
                    new_state["c_ckv"] = ckvn_f.reshape(B, 1, S, CKV)
                    new_state["c_krope"] = krp[:, :, :CR].reshape(B, 1, S, CR)
                    new_state["d_k"] = proj3[:, :, o_dk:o_dk + dqk].reshape(B, 1, S, DH, 2, DK)
                    new_state["d_v"] = proj3[:, :, o_dv:o_dv + dqk].reshape(B, 1, S, DH, DV)
                Sk = ckv_all.shape[1]
                knv = _matmul(ckv_all.reshape(B * Sk, CKV), wkv, out_dtype=BF16, tn_pref=2048).reshape(B, Sk, -1)
                zk = jnp.zeros_like(kr_all)
                kr2 = jnp.concatenate([kr_all, zk, zk, kr_all], axis=-1)
                c_out = _mla(qall, 0, qall, CH * CN // LANES, knv, kr2, H=CH, dn=CN, dr=CR)
                d_out = _diff(proj3, o_dq // DV, proj3, o_dk // DV, proj3, o_dv // DV, d_ctx,
                              lambda_q1[j], lambda_k1[j], lambda_q2[j], lambda_k2[j], g_subln[j], lam_init, H=DH, dk=DK)
                y = _matmul_parts([c_out.reshape(M, -1), d_out.reshape(M, -1)], w_out_odd[j].astype(BF16),
                                  out_dtype=BF16, tn_pref=1024)
            x, h = _resid_norm(x, row0, stride, resid=(y, g_post_mix[i], mods[i], G_M),
                               norm=(g_pre_ffn[i], mods[i], SC_F, SH_F))
            hmid = _ffn_up(h.reshape(M, D), S, w_up_b, conv_w, conv_b, i)
            f = _matmul(hmid, w_down_b, layer=i, out_dtype=BF16, tm_pref=512, tn_pref=512)
            if i + 1 < depth:
                x, h = _resid_norm(x, row0, stride, resid=(f, g_post_ffn[i], mods[i], G_F),
                                   norm=(g_pre_mix[i + 1], mods[i + 1], SC_M, SH_M))
            else:
                (x,) = _resid_norm(x, row0, stride, resid=(f, g_post_ffn[i], mods[i], G_F))
        grp["out"] = x

    return (groups[0]["out"], groups[1]["out"], new_state["a_k"], new_state["a_v"], new_state["c_ckv"],
            new_state["c_krope"], new_state["d_k"], new_state["d_v"])
```

```python
import functools
import math

import jax
import jax.numpy as jnp
from jax import lax
from jax.experimental import pallas as pl
from jax.experimental.pallas import tpu as pltpu

GRID_W = 64
Q_BLOCK = 128
WINDOW = 128
ROPE_BASE = 10000.0
EPS = 1e-6
NEG_INF = -1e30
POOL_WINDOWS = (2, 4, 8, 16)

LANES = 128
BF16_ROWS = 16
VMEM_LIMIT = 58 * 1024 * 1024
WIDE_TILE = 11 * LANES

BF16 = jnp.bfloat16
F32 = jnp.float32


def _tile(dim, pref, align):
    best = None
    t = align
    while t <= min(dim, pref):
        if dim % t == 0:
            best = t
        t += align
    return dim if best is None else best


def _params(*sem):
    return pltpu.CompilerParams(dimension_semantics=sem, vmem_limit_bytes=VMEM_LIMIT)


def _mods_kernel(c_ref, w_ref, b_ref, o_ref):
    c = c_ref[...]
    s = (c * (1.0 / (1.0 + jnp.exp(-c)))).astype(BF16)
    o_ref[...] = jnp.dot(s, w_ref[...].astype(BF16), preferred_element_type=F32) + b_ref[...]


def _mods(cond, w_mod, b_mod):
    depth, D, N = w_mod.shape
    R = cond.shape[0]
    tn = _tile(N, 512, LANES)
    return pl.pallas_call(
        _mods_kernel,
        grid=(depth, N // tn),
        in_specs=[
            pl.BlockSpec((R, D), lambda l, j: (0, 0)),
            pl.BlockSpec((None, D, tn), lambda l, j: (l, 0, j)),
            pl.BlockSpec((None, 1, tn), lambda l, j: (l, 0, j)),
        ],
        out_specs=pl.BlockSpec((None, R, tn), lambda l, j: (l, 0, j)),
        out_shape=jax.ShapeDtypeStruct((depth, R, N), F32),
        compiler_params=_params("parallel", "parallel"),
        name="mods",
    )(cond, w_mod, b_mod.reshape(depth, 1, N))


def _rms(x, g):
    return x * lax.rsqrt(jnp.mean(x * x, axis=-1, keepdims=True) + EPS) * g


def _resid_norm_kernel(*refs, has_resid, has_norm):
    refs = list(refs)
    x = refs.pop(0)[0]
    if has_resid:
        y_ref, gpost_ref, gate_ref = refs[:3]
        refs = refs[3:]
        x = x + gate_ref[...] * _rms(y_ref[0].astype(F32), gpost_ref[...])
    if has_norm:
        gpre_ref, sc_ref, sh_ref = refs[:3]
        refs = refs[3:]
    if has_resid:
        refs.pop(0)[0] = x
    if has_norm:
        h = _rms(x, gpre_ref[...]) * (1.0 + sc_ref[...]) + sh_ref[...]
        refs.pop(0)[0] = h.astype(BF16)


def _resid_norm(x, row0, row_stride, *, resid=None, norm=None):
    B, S, D = x.shape
    ts = _tile(S, 512, 8)
    xspec = pl.BlockSpec((1, ts, D), lambda b, s: (b, s, 0))
    gspec = pl.BlockSpec((1, D), lambda b, s: (0, 0))

    def mspec(col):
        return pl.BlockSpec((None, 1, D), lambda b, s: (row0 + b * row_stride, 0, col))

    args, in_specs, out_shape, out_specs = [x], [xspec], [], []
    if resid is not None:
        y, gpost, mods, gate_col = resid
        args += [y.reshape(B, S, D), gpost.reshape(1, D), mods]
        in_specs += [xspec, gspec, mspec(gate_col)]
        out_shape.append(jax.ShapeDtypeStruct((B, S, D), F32))
        out_specs.append(xspec)
    if norm is not None:
        gpre, mods, sc_col, sh_col = norm
        args += [gpre.reshape(1, D), mods, mods]
        in_specs += [gspec, mspec(sc_col), mspec(sh_col)]
        out_shape.append(jax.ShapeDtypeStruct((B, S, D), BF16))
        out_specs.append(xspec)
    outs = pl.pallas_call(
        functools.partial(_resid_norm_kernel, has_resid=resid is not None, has_norm=norm is not None),
        grid=(B, S // ts),
        in_specs=in_specs,
        out_specs=out_specs,
        out_shape=out_shape,
        compiler_params=_params("parallel", "parallel"),
        name="resid_norm",
    )(*args)
    return outs


def _mm_kernel(x_ref, w_ref, o_ref):
    o_ref[...] = jnp.dot(x_ref[...].astype(BF16), w_ref[...], preferred_element_type=F32).astype(o_ref.dtype)


def _mm_parts_kernel(*refs):
    o_ref = refs[-1]
    acc = None
    for x_ref, w_ref in zip(refs[0:-1:2], refs[1:-1:2]):
        d = jnp.dot(x_ref[...], w_ref[...], preferred_element_type=F32)
        acc = d if acc is None else acc + d
    o_ref[...] = acc.astype(o_ref.dtype)


def _matmul_parts(xs, w, *, out_dtype=F32, tm_pref=1024, tn_pref=512):
    M = xs[0].shape[0]
    N = w.shape[1]
    starts = [sum(x.shape[1] for x in xs[:n]) for n in range(len(xs))]
    if any(k0 % x.shape[1] for k0, x in zip(starts, xs)):
        return _matmul(jnp.concatenate(xs, axis=1), w, out_dtype=out_dtype, tm_pref=tm_pref, tn_pref=tn_pref)
    tm = _tile(M, tm_pref, BF16_ROWS)
    tn = _tile(N, tn_pref, LANES)
    in_specs, args, k0 = [], [], 0
    for x in xs:
        kp = x.shape[1]
        in_specs += [pl.BlockSpec((tm, kp), lambda i, j: (i, 0)),
                     pl.BlockSpec((kp, tn), lambda i, j, kb=k0 // kp: (kb, j))]
        args += [x, w]
        k0 += kp
    return pl.pallas_call(
        _mm_parts_kernel,
        grid=(M // tm, N // tn),
        in_specs=in_specs,
        out_specs=pl.BlockSpec((tm, tn), lambda i, j: (i, j)),
        out_shape=jax.ShapeDtypeStruct((M, N), out_dtype),
        compiler_params=_params("parallel", "parallel"),
        name="matmul_parts",
    )(*args)


def _rotate_slab(x, cos, sin, first, pair):
    partner = jnp.where(first, pltpu.roll(x, LANES - pair, 1), pltpu.roll(x, pair, 1))
    return x * cos + partner * sin


def _mm_rope_kernel(x_ref, w_ref, cos_ref, sin_ref, o_ref, *, pair, g0, g1):
    tm, tn = o_ref.shape
    spt = tn // LANES
    acc = jnp.dot(x_ref[...].astype(BF16), w_ref[...], preferred_element_type=F32)
    j = pl.program_id(1)
    lane = lax.broadcasted_iota(jnp.int32, (tm, LANES), 1)
    first = (lane % (2 * pair)) < pair

    def emit(rotated):
        for c in range(spt):
            x = acc[:, c * LANES:(c + 1) * LANES]
            if rotated[c]:
                x = _rotate_slab(x, cos_ref[...], sin_ref[...], first, pair)
            o_ref[:, c * LANES:(c + 1) * LANES] = x.astype(o_ref.dtype)

    full_lo, full_hi = -(-g0 // spt), g1 // spt
    mixed = [t for t in {g0 // spt, (g1 - 1) // spt} if not full_lo <= t < full_hi]
    is_full = (j >= full_lo) & (j < full_hi)
    is_mixed = functools.reduce(jnp.logical_or, [j == t for t in mixed], jnp.bool_(False))
    pl.when(is_full)(lambda: emit([True] * spt))
    for t in mixed:
        pl.when(j == t)(lambda t=t: emit([g0 <= t * spt + c < g1 for c in range(spt)]))
    pl.when(jnp.logical_not(is_full | is_mixed))(lambda: emit([False] * spt))


def _matmul(x, w, *, layer=None, out_dtype=F32, tm_pref=1024, tn_pref=512, rope=None):
    M, K = x.shape
    N = w.shape[-1]
    tn = _tile(N, tn_pref, LANES)
    if layer is None:
        wspec = pl.BlockSpec((K, tn), lambda i, j: (0, j))
    else:
        wspec = pl.BlockSpec((None, K, tn), lambda i, j: (layer, 0, j))
    if rope is None:
        tm = _tile(M, tm_pref, BF16_ROWS)
        kern, extra_specs, extra_args = _mm_kernel, [], []
    else:
        (cos, sin), pair, g0, g1, S = rope
        tm = _tile(S, tm_pref, BF16_ROWS)
        tps = S // tm
        tspec = pl.BlockSpec((tm, LANES), lambda i, j: (i % tps, 0))
        kern = functools.partial(_mm_rope_kernel, pair=pair, g0=g0, g1=g1)
        extra_specs, extra_args = [tspec, tspec], [cos, sin]
    return pl.pallas_call(
        kern,
        grid=(M // tm, N // tn),
        in_specs=[
            pl.BlockSpec((tm, K), lambda i, j: (i, 0)),
            wspec,
        ] + extra_specs,
        out_specs=pl.BlockSpec((tm, tn), lambda i, j: (i, j)),
        out_shape=jax.ShapeDtypeStruct((M, N), out_dtype),
        compiler_params=_params("parallel", "parallel"),
        name="matmul",
    )(x, w, *extra_args)


def _rms_cols_kernel(x_ref, g_ref, *o_refs):
    y = _rms(x_ref[...].astype(F32), g_ref[...])
    for o_ref in o_refs:
        o_ref[...] = y.astype(o_ref.dtype)


def _rms_cols(x, col_block, g, out_dtypes):
    M = x.shape[0]
    W = g.shape[-1]
    tm = _tile(M, 1024, BF16_ROWS)
    spec = pl.BlockSpec((tm, W), lambda i: (i, 0))
    return pl.pallas_call(
        _rms_cols_kernel,
        grid=(M // tm,),
        in_specs=[pl.BlockSpec((tm, W), lambda i: (i, col_block)), pl.BlockSpec((1, W), lambda i: (0, 0))],
        out_specs=[spec] * len(out_dtypes),
        out_shape=[jax.ShapeDtypeStruct((M, W), dt) for dt in out_dtypes],
        compiler_params=_params("parallel"),
        name="rms_cols",
    )(x, g.reshape(1, W))


def _rope_tables(S, width):
    d = width // 2
    half = d // 2
    inv = ROPE_BASE ** (-jnp.arange(half, dtype=F32) / half)
    t = jnp.arange(S)
    row = (t // GRID_W).astype(F32)[:, None] * inv[None, :]
    col = (t % GRID_W).astype(F32)[:, None] * inv[None, :]
    cos = jnp.concatenate([jnp.cos(row), jnp.cos(row), jnp.cos(col), jnp.cos(col)], axis=-1)
    sin = jnp.concatenate([-jnp.sin(row), jnp.sin(row), -jnp.sin(col), jnp.sin(col)], axis=-1)
    reps = LANES // width
    return jnp.tile(cos, (1, reps)), jnp.tile(sin, (1, reps))


LOG2E = math.log2(math.e)
ATTN_ROWS = 256
ATTN_GROUPS = 8


def _heads_per_step(n_heads, S, *col_block_offsets):
    if S > ATTN_ROWS:
        return 1
    return max(d for d in range(1, n_heads + 1)
               if n_heads % d == 0 and all(off % d == 0 for off in col_block_offsets))


def _logit_scale(d):
    return d ** -0.5 * LOG2E


def _dot_nt(a, b):
    return lax.dot_general(a, b, (((1,), (1,)), ((), ())), preferred_element_type=F32)


def _gqa_kernel(*refs, G, hd, tq, nb, kvb, S, windowed):
    if windowed:
        q_ref, k_ref, v_ref, sink_ref, ck_ref, cv_ref, o_ref = refs
    else:
        q_ref, k_ref, v_ref, sink_ref, o_ref = refs
    rows = G * tq
    if windowed:
        span = 3 * Q_BLOCK
        rel0 = (lax.broadcasted_iota(jnp.int32, (rows, span), 1)
                - lax.broadcasted_iota(jnp.int32, (rows, span), 0) % tq)
    for kh in range(kvb):
        kcols = slice(kh * hd, (kh + 1) * hd)
        q0 = kh * G * hd
        sink = jnp.concatenate([jnp.broadcast_to(sink_ref[kh, g:g + 1, :], (tq, LANES)) for g in range(G)],
                               axis=0) * LOG2E
        for blk in range(nb):
            r0 = blk * tq
            q = jnp.concatenate([q_ref[0, r0:r0 + tq, q0 + g * hd:q0 + (g + 1) * hd] for g in range(G)],
                                axis=0).astype(BF16)
            if windowed:
                n = pl.program_id(2) * nb + blk
                start = pl.multiple_of(jnp.clip(n * tq - Q_BLOCK, 0, S - span), Q_BLOCK)
                kw = k_ref[0, pl.ds(start, span), kcols].astype(BF16)
                vw = v_ref[0, pl.ds(start, span), kcols].astype(BF16)
                s = jnp.where(jnp.abs(rel0 + (start - n * tq)) <= WINDOW, _dot_nt(q, kw), NEG_INF)
                sc = _dot_nt(q, ck_ref[0].astype(BF16))
                m = jnp.maximum(jnp.max(s, axis=-1, keepdims=True), jnp.max(sc, axis=-1, keepdims=True))
            else:
                kw = k_ref[0, :, kcols].astype(BF16)
                vw = v_ref[0, :, kcols].astype(BF16)
                s = _dot_nt(q, kw)
                m = jnp.max(s, axis=-1, keepdims=True)
            p = jnp.exp2(s - m)
            l = jnp.sum(p, axis=-1, keepdims=True)
            acc = jnp.dot(p.astype(BF16), vw, preferred_element_type=F32)
            if windowed:
                pc = jnp.exp2(sc - m)
                l = l + jnp.sum(pc, axis=-1, keepdims=True)
                acc = acc + jnp.dot(pc.astype(BF16), cv_ref[0].astype(BF16), preferred_element_type=F32)
            out = acc / (l + jnp.exp2(sink - m))
            for g in range(G):
                o_ref[0, r0:r0 + tq, q0 + g * hd:q0 + (g + 1) * hd] = out[g * tq:(g + 1) * tq].astype(o_ref.dtype)


def _gqa(q, q_cb0, k, k_cb0, v, v_cb0, sink, ctx, *, KV, G, hd):
    B, S, _ = q.shape
    windowed = ctx is not None
    tq = Q_BLOCK if windowed else _tile(S, 256, BF16_ROWS)
    nb = max(n for n in (1, 2, 4, 8) if S % (n * tq) == 0) if windowed else 1
    kvb = 1 if windowed else _heads_per_step(KV, S, q_cb0, k_cb0, v_cb0)
    ts = nb * tq
    sink3 = jnp.broadcast_to(sink.astype(F32).reshape(KV, G, 1), (KV, G, LANES))
    in_specs = [
        pl.BlockSpec((1, ts, G * hd * kvb), lambda b, h, n: (b, n, q_cb0 // kvb + h)),
        pl.BlockSpec((1, S, hd * kvb), lambda b, h, n: (b, 0, k_cb0 // kvb + h)),
        pl.BlockSpec((1, S, hd * kvb), lambda b, h, n: (b, 0, v_cb0 // kvb + h)),
        pl.BlockSpec((kvb, G, LANES), lambda b, h, n: (h, 0, 0)),
    ]
    args = [q, k, v, sink3]
    if windowed:
        ck, cv = ctx
        P = ck.shape[1]
        cspec = pl.BlockSpec((1, P, hd), lambda b, h, n: (b, 0, h))
        in_specs += [cspec, cspec]
        args += [ck, cv]
    return pl.pallas_call(
        functools.partial(_gqa_kernel, G=G, hd=hd, tq=tq, nb=nb, kvb=kvb, S=S, windowed=windowed),
        grid=(B, KV // kvb, S // ts),
        in_specs=in_specs,
        out_specs=pl.BlockSpec((1, ts, G * hd * kvb), lambda b, h, n: (b, n, h)),
        out_shape=jax.ShapeDtypeStruct((B, S, KV * G * hd), BF16),
        compiler_params=_params("parallel", "parallel", "parallel"),
        name="gqa_attention",
    )(*args)


POOL_PAD = 16


def _pool_kernel(u_ref, w_ref, ps_ref, o_ref, pad_ref, *, S):
    grp = pl.program_id(1)
    u = u_ref[0].astype(F32)
    C = u.shape[-1]
    pad_ref[0:POOL_PAD, :] = jnp.zeros((POOL_PAD, C), F32)
    pad_ref[POOL_PAD + S:2 * POOL_PAD + S, :] = jnp.zeros((POOL_PAD, C), F32)
    pad_ref[POOL_PAD:POOL_PAD + S, :] = u
    t = lax.broadcasted_iota(jnp.int32, (S, 1), 0)
    for gi, win in enumerate(POOL_WINDOWS):
        @pl.when(grp == gi)
        def _(win=win):
            half = win // 2
            total = pad_ref[POOL_PAD - half:POOL_PAD - half + S, :]
            for off in range(-half + 1, half):
                total = total + pad_ref[POOL_PAD + off:POOL_PAD + off + S, :]
            count = (jnp.clip(t + half, 0, S) - jnp.clip(t - half, 0, S)).astype(F32)
            pooled = (total / count - u).astype(BF16)
            y = jnp.dot(pooled, w_ref[0].astype(BF16), preferred_element_type=F32) * ps_ref[...]
            o_ref[0] = y.astype(o_ref.dtype)


def _pool(u, u_cb0, w_pool, pool_scale):
    B, S, _ = u.shape
    NG, Cg, _ = w_pool.shape
    assert NG == len(POOL_WINDOWS)
    return pl.pallas_call(
        functools.partial(_pool_kernel, S=S),
        grid=(B, NG),
        in_specs=[
            pl.BlockSpec((1, S, Cg), lambda b, g: (b, 0, u_cb0 + g)),
            pl.BlockSpec((1, Cg, Cg), lambda b, g: (g, 0, 0)),
            pl.BlockSpec((1, Cg), lambda b, g: (0, g)),
        ],
        out_specs=pl.BlockSpec((1, S, Cg), lambda b, g: (b, 0, g)),
        out_shape=jax.ShapeDtypeStruct((B, S, NG * Cg), BF16),
        scratch_shapes=[pltpu.VMEM((S + 2 * POOL_PAD, Cg), F32)],
        compiler_params=_params("parallel", "parallel"),
        name="pool_mixer",
    )(u, w_pool, pool_scale.reshape(1, NG * Cg))


def _mla_kernel(qn_ref, qr_ref, kn_ref, kr_ref, v_ref, o_ref, *, dn, hpb):
    tq = qn_ref.shape[1]
    ones = jnp.ones((kn_ref.shape[1], LANES), BF16)
    for r0 in range(0, tq, ATTN_ROWS):
        rows = slice(r0, min(r0 + ATTN_ROWS, tq))
        for hp in range(hpb):
            qr = qr_ref[0, rows, hp * LANES:(hp + 1) * LANES].astype(BF16)
            for e in range(2):
                cols = slice((2 * hp + e) * dn, (2 * hp + e + 1) * dn)
                q = jnp.concatenate([qn_ref[0, rows, cols].astype(BF16), qr], axis=-1)
                k = jnp.concatenate([kn_ref[0, :, cols], kr_ref[0, :, e * LANES:(e + 1) * LANES]], axis=-1)
                s = _dot_nt(q, k)
                p = jnp.exp2(s - jnp.max(s, axis=-1, keepdims=True)).astype(BF16)
                acc = jnp.dot(p, jnp.concatenate([v_ref[0, :, cols], ones], axis=-1), preferred_element_type=F32)
                o_ref[0, rows, cols] = (acc[:, 0:dn] / acc[:, dn:2 * dn]).astype(o_ref.dtype)


def _mla(qn, qn_cb0, qr, qr_cb0, knv, kr2, *, H, dn, dr):
    B, S, _ = qn.shape
    Sk = knv.shape[1]
    assert dn == LANES and 2 * dr == LANES
    tq = _tile(S, ATTN_GROUPS * ATTN_ROWS, BF16_ROWS)
    HP = H // 2
    hpb = _heads_per_step(HP, S, qn_cb0, qr_cb0)
    nh = HP // hpb
    return pl.pallas_call(
        functools.partial(_mla_kernel, dn=dn, hpb=hpb),
        grid=(B, nh, S // tq),
        in_specs=[
            pl.BlockSpec((1, tq, 2 * dn * hpb), lambda b, h, n: (b, n, qn_cb0 // hpb + h)),
            pl.BlockSpec((1, tq, LANES * hpb), lambda b, h, n: (b, n, qr_cb0 // hpb + h)),
            pl.BlockSpec((1, Sk, 2 * dn * hpb), lambda b, h, n: (b, 0, h)),
            pl.BlockSpec((1, Sk, 2 * LANES), lambda b, h, n: (b, 0, 0)),
            pl.BlockSpec((1, Sk, 2 * dn * hpb), lambda b, h, n: (b, 0, nh + h)),
        ],
        out_specs=pl.BlockSpec((1, tq, 2 * dn * hpb), lambda b, h, n: (b, n, h)),
        out_shape=jax.ShapeDtypeStruct((B, S, H * dn), BF16),
        compiler_params=_params("parallel", "parallel", "parallel"),
        name="mla_attention",
    )(qn, qr, knv, kr2, knv)


def _diff_kernel(*refs, dk, lam_init, has_ctx, hb):
    if has_ctx:
        q_ref, k_ref, v_ref, ck_ref, cv_ref, l1_ref, l2_ref, gs_ref, o_ref = refs
        key_refs, val_refs = (k_ref, ck_ref), (v_ref, cv_ref)
    else:
        q_ref, k_ref, v_ref, l1_ref, l2_ref, gs_ref, o_ref = refs
        key_refs, val_refs = (k_ref,), (v_ref,)
    lam = (jnp.exp(jnp.sum(l1_ref[0:1, :] * l1_ref[1:2, :], axis=-1, keepdims=True))
           - jnp.exp(jnp.sum(l2_ref[0:1, :] * l2_ref[1:2, :], axis=-1, keepdims=True)) + lam_init)
    tq = q_ref.shape[1]
    dv = 2 * dk
    for r0 in range(0, tq, ATTN_ROWS):
        rows = slice(r0, min(r0 + ATTN_ROWS, tq))
        for hh in range(hb):
            out = None
            for mi in range(2):
                cols = slice(hh * dv + mi * dk, hh * dv + (mi + 1) * dk)
                q = q_ref[0, rows, cols].astype(BF16)
                ss = [_dot_nt(q, kr[0, :, cols].astype(BF16)) for kr in key_refs]
                m = functools.reduce(jnp.maximum, [jnp.max(s, axis=-1, keepdims=True) for s in ss])
                es = [jnp.exp2(s - m) for s in ss]
                tot = functools.reduce(jnp.add, [jnp.sum(e, axis=-1, keepdims=True) for e in es])
                pv = functools.reduce(jnp.add, [
                    jnp.dot(e.astype(BF16), vr[0, :, hh * dv:(hh + 1) * dv].astype(BF16), preferred_element_type=F32)
                    for e, vr in zip(es, val_refs)])
                out = pv / tot if mi == 0 else out - pv * (lam / tot)
            o_ref[0, rows, hh * dv:(hh + 1) * dv] = (_rms(out, gs_ref[...]) * (1.0 - lam_init)).astype(o_ref.dtype)


def _diff(q, q_cb0, k, k_cb0, v, v_cb0, ctx, lam_q1, lam_k1, lam_q2, lam_k2, g_subln, lam_init, *, H, dk):
    B, S, _ = q.shape
    dv = 2 * dk
    tq = _tile(S, ATTN_GROUPS * ATTN_ROWS, BF16_ROWS)
    l1 = jnp.stack([lam_q1, lam_k1]).astype(F32)
    l2 = jnp.stack([lam_q2, lam_k2]).astype(F32)
    lspec = pl.BlockSpec((2, dk), lambda b, h, n: (0, 0))
    hb = 1 if ctx is not None else _heads_per_step(H, S, q_cb0, k_cb0, v_cb0)
    in_specs = [
        pl.BlockSpec((1, tq, dv * hb), lambda b, h, n: (b, n, q_cb0 // hb + h)),
        pl.BlockSpec((1, S, dv * hb), lambda b, h, n: (b, 0, k_cb0 // hb + h)),
        pl.BlockSpec((1, S, dv * hb), lambda b, h, n: (b, 0, v_cb0 // hb + h)),
    ]
    args = [q, k, v]
    if ctx is not None:
        P = ctx[0].shape[1]
        cspec = pl.BlockSpec((1, P, dv), lambda b, h, n: (b, 0, h))
        in_specs += [cspec, cspec]
        args += list(ctx)
    in_specs += [lspec, lspec, pl.BlockSpec((1, dv), lambda b, h, n: (0, 0))]
    args += [l1, l2, g_subln.reshape(1, dv)]
    return pl.pallas_call(
        functools.partial(_diff_kernel, dk=dk, lam_init=lam_init, has_ctx=ctx is not None, hb=hb),
        grid=(B, H // hb, S // tq),
        in_specs=in_specs,
        out_specs=pl.BlockSpec((1, tq, dv * hb), lambda b, h, n: (b, n, h)),
        out_shape=jax.ShapeDtypeStruct((B, S, H * dv), BF16),
        compiler_params=_params("parallel", "parallel", "parallel"),
        name="diff_attention",
    )(*args)


Z_PAD = 8
FFN_ROW_CHUNK = 32
FFN_DOT_ROWS = 1024


def _ffn_up_kernel(h_ref, wg_ref, wv_ref, cwg_ref, cwv_ref, cbg_ref, cbv_ref, o_ref, z_ref, *, tm, tn, S, rc, rm):
    zero = jnp.zeros((Z_PAD, 2 * tn), F32)
    z_ref[0:Z_PAD, :] = zero
    z_ref[Z_PAD + tm:2 * Z_PAD + tm, :] = zero
    for r0 in range(0, tm, rm):
        h = h_ref[r0:r0 + rm, :]
        z_ref[Z_PAD + r0:Z_PAD + r0 + rm, 0:tn] = jnp.dot(h, wg_ref[...], preferred_element_type=F32)
        z_ref[Z_PAD + r0:Z_PAD + r0 + rm, tn:2 * tn] = jnp.dot(h, wv_ref[...], preferred_element_type=F32)

    cw = jnp.concatenate([cwg_ref[...], cwv_ref[...]], axis=1)
    cb = jnp.concatenate([cbg_ref[...], cbv_ref[...]], axis=1)
    span = rc + 2 * Z_PAD
    for r0 in range(0, tm, rc):
        zf = z_ref[r0:r0 + span, :]
        zc = zf[Z_PAD:Z_PAD + rc]
        zp = pltpu.roll(zf, 1, 0)[Z_PAD:Z_PAD + rc]
        zn = pltpu.roll(zf, span - 1, 0)[Z_PAD:Z_PAD + rc]
        if tm > S:
            pos = (r0 + lax.broadcasted_iota(jnp.int32, (rc, 1), 0)) % S
            zp = jnp.where(pos != 0, zp, 0.0)
            zn = jnp.where(pos != S - 1, zn, 0.0)
        y = zp * cw[0:1, :] + zc * cw[1:2, :] + zn * cw[2:3, :] + cb
        g = y[:, 0:tn]
        sig = 1.0 / (1.0 + jnp.exp2(g * (-LOG2E)))
        o_ref[r0:r0 + rc, :] = (g * sig * y[:, tn:2 * tn]).astype(o_ref.dtype)


def _ffn_up(h, S, w_up, conv_w, conv_b, layer):
    M, D = h.shape
    L = w_up.shape[0]
    F = w_up.shape[2] // 2
    tm = _tile(M, max(2048, S), S)
    assert tm % S == 0
    tn = _tile(F, 256, LANES)
    nj = F // tn
    rc = _tile(tm, FFN_ROW_CHUNK, 8)
    rm = _tile(tm, FFN_DOT_ROWS, BF16_ROWS)
    cb = conv_b.reshape(L, 1, 2 * F)
    gate_tile = lambda i, j: (layer, 0, j)
    val_tile = lambda i, j: (layer, 0, nj + j)
    return pl.pallas_call(
        functools.partial(_ffn_up_kernel, tm=tm, tn=tn, S=S, rc=rc, rm=rm),
        grid=(M // tm, nj),
        in_specs=[
            pl.BlockSpec((tm, D), lambda i, j: (i, 0)),
            pl.BlockSpec((None, D, tn), gate_tile),
            pl.BlockSpec((None, D, tn), val_tile),
            pl.BlockSpec((None, 3, tn), gate_tile),
            pl.BlockSpec((None, 3, tn), val_tile),
            pl.BlockSpec((None, 1, tn), gate_tile),
            pl.BlockSpec((None, 1, tn), val_tile),
        ],
        out_specs=pl.BlockSpec((tm, tn), lambda i, j: (i, j)),
        out_shape=jax.ShapeDtypeStruct((M, F), BF16),
        scratch_shapes=[pltpu.VMEM((tm + 2 * Z_PAD, 2 * tn), F32)],
        compiler_params=_params("parallel", "parallel"),
        name="ffn_up_conv_gate",
    )(h, w_up, w_up, conv_w, conv_w, cb, cb)


def kernel(x_prompt, x_sample, cache_a_k, cache_a_v, cache_c_ckv, cache_c_krope, cache_d_k, cache_d_v, c, c_ctx, w_mod, b_mod, g_pre_mix, g_post_mix, g_pre_ffn, g_post_ffn, w_in_even, w_out_even, a_sink, w_pool, pool_scale, w_in_odd, w_out_odd, g_q_norm, w_uq, g_kv_norm, w_uk, w_uv, lambda_q1, lambda_k1, lambda_q2, lambda_k2, g_subln, w_up, conv_w, conv_b, w_down):
    B1, S1, D = x_prompt.shape
    B2, S2, _ = x_sample.shape
    depth = w_mod.shape[0]
    P = cache_a_k.shape[2]
    KV, hd = cache_a_k.shape[3], cache_a_k.shape[4]
    AH = a_sink.shape[1]
    G = AH // KV
    NG, Cg = w_pool.shape[1], w_pool.shape[2]
    qa, kva, pool_dim = AH * hd, KV * hd, NG * Cg
    CQ = g_q_norm.shape[1]
    CKV, CH, CN = w_uk.shape[1], w_uk.shape[2], w_uk.shape[3]
    CR = cache_c_krope.shape[3]
    CV = w_uv.shape[3]
    DH, DK = cache_d_k.shape[3], cache_d_k.shape[5]
    DV = 2 * DK
    dqk = DH * 2 * DK
    assert hd == LANES and CN == LANES and CV == CN and 2 * CR == LANES and DK == LANES

    R = -(-(B2 + 1) // 8) * 8
    cond = jnp.zeros((R, D), F32).at[:B2].set(c).at[B2].set(c_ctx)
    mods = _mods(cond, w_mod, b_mod).reshape(depth, R, 1, 6 * D)
    SH_M, SC_M, G_M, SH_F, SC_F, G_F = range(6)

    groups = [
        dict(x=x_prompt, B=B1, S=S1, row0=B2, stride=0, latent=False),
        dict(x=x_sample, B=B2, S=S2, row0=0, stride=1, latent=True),
    ]
    rope_hd = _rope_tables(S2, hd)
    rope_cr = _rope_tables(S2, CR)
    w_up_b = w_up.astype(BF16)
    w_down_b = w_down.astype(BF16)

    new_state = {}
    for grp in groups:
        B, S, latent = grp["B"], grp["S"], grp["latent"]
        M = B * S
        row0, stride = grp["row0"], grp["stride"]
        x = grp["x"]
        (h,) = _resid_norm(x, row0, stride, norm=(g_pre_mix[0], mods[0], SC_M, SH_M))
        for i in range(depth):
            j = i // 2
            hf = h.reshape(M, D)
            pdt = BF16 if latent else F32
            proj_tn = WIDE_TILE if latent else 512

            def rope(tables, pair, col0, col1):
                return (tables, pair, col0 // LANES, col1 // LANES, S) if latent else None

            if i % 2 == 0:
                wi = w_in_even[j]
                w_in = jnp.concatenate([wi[:, :qa] * _logit_scale(hd), wi[:, qa:]], axis=1).astype(BF16)
                proj = _matmul(hf, w_in, out_dtype=pdt, tn_pref=proj_tn,
                               rope=rope(rope_hd, hd // 4, 0, qa + kva)).reshape(B, S, -1)
                if latent:
                    ctx = (cache_a_k[:, j].reshape(B, P, kva), cache_a_v[:, j].reshape(B, P, kva))
                else:
                    ctx = None
                    new_state["a_k"] = proj[:, :, qa:qa + kva].reshape(B, 1, S, KV, hd)
                    new_state["a_v"] = proj[:, :, qa + kva:qa + 2 * kva].reshape(B, 1, S, KV, hd)
                a = _gqa(proj, 0, proj, qa // hd, proj, (qa + kva) // hd, a_sink[j], ctx, KV=KV, G=G, hd=hd)
                bmix = _pool(proj, (qa + 2 * kva) // Cg, w_pool[j], pool_scale[j])
                y = _matmul_parts([a.reshape(M, qa), bmix.reshape(M, pool_dim)], w_out_even[j].astype(BF16),
                                  out_dtype=BF16, tn_pref=1024)
            else:
                lam_init = 0.8 - 0.6 * math.exp(-0.3 * i)
                wi = w_in_odd[j]
                c1, c2 = CQ + CKV, CQ + CKV + CR
                w_main = jnp.concatenate([wi[:, :c1], wi[:, c2:c2 + dqk] * _logit_scale(DK), wi[:, c2 + dqk:]],
                                         axis=1).astype(BF16)
                w_kr = jnp.concatenate([wi[:, c1:c2], jnp.zeros((D, LANES - CR), F32)], axis=1).astype(BF16)
                o_dq, o_dk, o_dv = c1, c1 + dqk, c1 + 2 * dqk
                proj = _matmul(hf, w_main, out_dtype=pdt, tn_pref=proj_tn, rope=rope(rope_hd, hd // 4, o_dq, o_dv))
                krp = _matmul(hf, w_kr, out_dtype=pdt, tn_pref=LANES,
                              rope=rope(rope_cr, CR // 4, 0, LANES)).reshape(B, S, LANES)
                (cqn,) = _rms_cols(proj, 0, g_q_norm[j], [BF16])
                ckvn_f, ckvn_b = _rms_cols(proj, CQ // CKV, g_kv_norm[j], [F32, BF16])
                wq = w_uq[j].reshape(CQ, CH, CN + CR)
                wq = jnp.concatenate([wq[:, :, :CN].reshape(CQ, CH * CN), wq[:, :, CN:].reshape(CQ, CH * CR)], axis=1)
                wq = wq * _logit_scale(CN + CR)
                qall = _matmul(cqn, wq.astype(BF16), out_dtype=pdt, tn_pref=1024,
                               rope=rope(rope_cr, CR // 4, CH * CN, CH * (CN + CR))).reshape(B, S, CH * (CN + CR))
                wkv = jnp.concatenate([w_uk[j].reshape(CKV, CH * CN), w_uv[j].reshape(CKV, CH * CV)], axis=1).astype(BF16)
                proj3 = proj.reshape(B, S, -1)
                if latent:
                    ckv_all = jnp.concatenate([ckvn_b.reshape(B, S, CKV), cache_c_ckv[:, j].astype(BF16)], axis=1)
                    kr_all = jnp.concatenate([krp[:, :, :CR], cache_c_krope[:, j].astype(BF16)], axis=1)
                    d_ctx = (cache_d_k[:, j].reshape(B, P, dqk), cache_d_v[:, j].reshape(B, P, DH * DV))
                else:
                    ckv_all = ckvn_b.reshape(B, S, CKV)
                    kr_all = krp[:, :, :CR].astype(BF16)
                    d_ctx = None
                    new_state["c_ckv"] = ckvn_f.reshape(B, 1, S, CKV)
                    new_state["c_krope"] = krp[:, :, :CR].reshape(B, 1, S, CR)
                    new_state["d_k"] = proj3[:, :, o_dk:o_dk + dqk].reshape(B, 1, S, DH, 2, DK)
                    new_state["d_v"] = proj3[:, :, o_dv:o_dv + dqk].reshape(B, 1, S, DH, DV)
                Sk = ckv_all.shape[1]
                knv = _matmul(ckv_all.reshape(B * Sk, CKV), wkv, out_dtype=BF16, tn_pref=2048).reshape(B, Sk, -1)
                zk = jnp.zeros_like(kr_all)
                kr2 = jnp.concatenate([kr_all, zk, zk, kr_all], axis=-1)
                c_out = _mla(qall, 0, qall, CH * CN // LANES, knv, kr2, H=CH, dn=CN, dr=CR)
                d_out = _diff(proj3, o_dq // DV, proj3, o_dk // DV, proj3, o_dv // DV, d_ctx,
                              lambda_q1[j], lambda_k1[j], lambda_q2[j], lambda_k2[j], g_subln[j], lam_init, H=DH, dk=DK)
                y = _matmul_parts([c_out.reshape(M, -1), d_out.reshape(M, -1)], w_out_odd[j].astype(BF16),
                                  out_dtype=BF16, tn_pref=1024)
            x, h = _resid_norm(x, row0, stride, resid=(y, g_post_mix[i], mods[i], G_M),
                               norm=(g_pre_ffn[i], mods[i], SC_F, SH_F))
            hmid = _ffn_up(h.reshape(M, D), S, w_up_b, conv_w, conv_b, i)
            f = _matmul(hmid, w_down_b, layer=i, out_dtype=BF16, tm_pref=512, tn_pref=512)
            if i + 1 < depth:
                x, h = _resid_norm(x, row0, stride, resid=(f, g_post_ffn[i], mods[i], G_F),
                                   norm=(g_pre_mix[i + 1], mods[i + 1], SC_M, SH_M))
            else:
                (x,) = _resid_norm(x, row0, stride, resid=(f, g_post_ffn[i], mods[i], G_F))
        grp["out"] = x

    return (groups[0]["out"], groups[1]["out"], new_state["a_k"], new_state["a_v"], new_state["c_ckv"],
            new_state["c_krope"], new_state["d_k"], new_state["d_v"])
```
